```python
import math
import jax, jax.numpy as jnp
from jax import lax
import numpy as np

D_MODEL = 4096
BATCH = 1
SEQ = 8192
DEPTH = 1
DEC_BATCH = 16
DEC_SEQ = 16
PAST_LEN = 2048

CHUNK = 64
D_MIX = D_MODEL
D_MLSTM = D_MIX // 2
D_S5 = D_MIX - D_MLSTM
M_HEADS = 4
M_DV = D_MLSTM // M_HEADS
M_DQK = M_DV // 2
CONV_W = 4
S5_GROUP = 16
S5_GROUPS = D_S5 // S5_GROUP
S5_STATE = 64
D_FF = ((8 * D_MODEL // 3 + 255) // 256) * 256
D_IN = 3 * D_MLSTM + D_S5 + 2 * M_HEADS
RMS_EPS = 1e-6

kernel_name = 'hybrid_mlstm_s5_stream_step'


def rmsnorm(x, g):
    xf = x.astype(jnp.float32)
    y = xf * lax.rsqrt(jnp.mean(xf * xf, axis=-1, keepdims=True) + RMS_EPS)
    return (y * g.astype(jnp.float32)).astype(x.dtype)


def swiglu(x, w_gate, w_up, w_down):
    return (jax.nn.silu(x @ w_gate) * (x @ w_up)) @ w_down


def causal_conv(u, buf, w, b):
    L = u.shape[1]
    xp = jnp.concatenate([buf.astype(u.dtype), u], axis=1)
    y = b + xp[:, 0:L] * w[0]
    for j in range(1, CONV_W):
        y = y + xp[:, j:j + L] * w[j]
    return y, xp[:, L:]


def to_chunks(a, lc):
    bsz, nh, L = a.shape[:3]
    a = a.reshape((bsz, nh, L // lc, lc) + a.shape[3:])
    return jnp.moveaxis(a, 2, 0)


def from_chunks(a):
    a = jnp.moveaxis(a, 0, 2)
    return a.reshape((a.shape[0], a.shape[1], a.shape[2] * a.shape[3]) + a.shape[4:])


def mlstm_chunk(carry, inp):
    c0, n0, m0 = carry
    q, k, v, ig, lf = inp
    lc = q.shape[2]
    b = jnp.cumsum(lf, axis=-1)
    causal = jnp.tril(jnp.ones((lc, lc), dtype=bool))
    log_d = jnp.where(causal, b[..., :, None] - b[..., None, :] + ig[..., None, :], -jnp.inf)
    inter = b + m0[..., None]
    m_t = jnp.maximum(inter, jnp.max(log_d, axis=-1))
    s = jnp.einsum('bhtd,bhsd->bhts', q, k) * jnp.exp(log_d - m_t[..., None])
    w_inter = jnp.exp(inter - m_t)
    num = w_inter[..., None] * jnp.einsum('bhtd,bhde->bhte', q, c0) + jnp.einsum('bhts,bhse->bhte', s, v)
    den = w_inter * jnp.einsum('bhtd,bhd->bht', q, n0) + jnp.sum(s, axis=-1)
    h = num / jnp.maximum(jnp.abs(den), jnp.exp(-m_t))[..., None]
    b_last = b[..., -1]
    log_w = b_last[..., None] - b + ig
    m_new = jnp.maximum(b_last + m0, jnp.max(log_w, axis=-1))
    w = jnp.exp(log_w - m_new[..., None])
    decay = jnp.exp(b_last + m0 - m_new)
    c_new = decay[..., None, None] * c0 + jnp.einsum('bhs,bhsd,bhse->bhde', w, k, v)
    n_new = decay[..., None] * n0 + jnp.einsum('bhs,bhsd->bhd', w, k)
    return (c_new, n_new, m_new), h


def mlstm(q, k, v, ig, lf, c0, n0, m0):
    lc = min(CHUNK, q.shape[2])
    xs = (to_chunks(q, lc), to_chunks(k, lc), to_chunks(v, lc), to_chunks(ig, lc), to_chunks(lf, lc))
    (c, n, m), h = lax.scan(mlstm_chunk, (c0, n0, m0), xs)
    return from_chunks(h), c, n, m


def complex_combine(e1, e2):
    a1r, a1i, b1r, b1i = e1
    a2r, a2i, b2r, b2i = e2
    return (a1r * a2r - a1i * a2i,
            a1r * a2i + a1i * a2r,
            a2r * b1r - a2i * b1i + b2r,
            a2r * b1i + a2i * b1r + b2i)


def s5_discretize(a_re, a_im, log_dt, b_re, b_im):
    dt = jnp.exp(log_dt)[:, None]
    mag = jnp.exp(a_re * dt)
    ab_re = mag * jnp.cos(a_im * dt)
    ab_im = mag * jnp.sin(a_im * dt)
    den = a_re * a_re + a_im * a_im
    inv_re = a_re / den
    inv_im = -a_im / den
    f_re = ab_re - 1.0
    f_im = ab_im
    q_re = f_re * inv_re - f_im * inv_im
    q_im = f_re * inv_im + f_im * inv_re
    bb_re = q_re[..., None] * b_re - q_im[..., None] * b_im
    bb_im = q_re[..., None] * b_im + q_im[..., None] * b_re
    return ab_re, ab_im, bb_re, bb_im


def s5_scan(u, h_re, h_im, ab_re, ab_im, bb_re, bb_im, c_re, c_im, d):
    bsz, L = u.shape[:2]
    lc = min(CHUNK, L)
    uc = jnp.moveaxis(u.reshape(bsz, L // lc, lc, S5_GROUPS, S5_GROUP), 1, 0)

    def step(carry, u_c):
        hr, hi = carry
        bu_re = jnp.einsum('blgj,gpj->blgp', u_c, bb_re)
        bu_im = jnp.einsum('blgj,gpj->blgp', u_c, bb_im)
        bu_re = bu_re.at[:, 0].add(ab_re * hr - ab_im * hi)
        bu_im = bu_im.at[:, 0].add(ab_re * hi + ab_im * hr)
        a_r = jnp.broadcast_to(ab_re, bu_re.shape)
        a_i = jnp.broadcast_to(ab_im, bu_re.shape)
        _, _, sr, si = lax.associative_scan(complex_combine, (a_r, a_i, bu_re, bu_im), axis=1)
        y = (jnp.einsum('blgp,gjp->blgj', sr, c_re) - jnp.einsum('blgp,gjp->blgj', si, c_im)
             + d * u_c)
        return (sr[:, -1], si[:, -1]), y

    (hr, hi), y = lax.scan(step, (h_re, h_im), uc)
    y = jnp.moveaxis(y, 0, 1).reshape(bsz, L, D_S5)
    return y, hr, hi


def layer(x, m_c, m_n, m_m, conv_buf, s5_re, s5_im, p):
    dt = x.dtype
    bsz, L, _ = x.shape
    f32 = jnp.float32
    h = rmsnorm(x, p['ff1_norm_pre'])
    x = x + 0.5 * rmsnorm(swiglu(h, p['ff1_w_gate'], p['ff1_w_up'], p['ff1_w_down']), p['ff1_norm_post'])
    h = rmsnorm(x, p['mix_norm_pre'])
    proj = h @ p['w_in']
    u_m, v, o, u_s, ig, fg = jnp.split(
        proj, [D_MLSTM, 2 * D_MLSTM, 3 * D_MLSTM, 3 * D_MLSTM + D_S5, 3 * D_MLSTM + D_S5 + M_HEADS], axis=-1)
    cv, conv_new = causal_conv(u_m, conv_buf, p['mlstm_conv_w'], p['mlstm_conv_b'])
    cv = jax.nn.silu(cv).reshape(bsz, L, M_HEADS, M_DV)
    q = jnp.einsum('blhd,hde->bhle', cv, p['mlstm_w_q']).astype(f32)
    k = jnp.einsum('blhd,hde->bhle', cv, p['mlstm_w_k']).astype(f32) * (M_DQK ** -0.5)
    vh = v.reshape(bsz, L, M_HEADS, M_DV).transpose(0, 2, 1, 3).astype(f32)
    ig = (ig + p['mlstm_b_i']).astype(f32).transpose(0, 2, 1)
    lf = jax.nn.log_sigmoid((fg + p['mlstm_b_f']).astype(f32)).transpose(0, 2, 1)
    hm, c_new, n_new, m_new = mlstm(q, k, vh, ig, lf, m_c.astype(f32), m_n.astype(f32), m_m.astype(f32))
    hm = rmsnorm(hm, p['mlstm_head_norm'][:, None, :])
    hm = hm.transpose(0, 2, 1, 3).reshape(bsz, L, D_MLSTM).astype(dt) * jax.nn.sigmoid(o)
    ab_re, ab_im, bb_re, bb_im = s5_discretize(
        p['s5_a_re'].astype(f32), p['s5_a_im'].astype(f32), p['s5_log_dt'].astype(f32),
        p['s5_b_re'].astype(f32), p['s5_b_im'].astype(f32))
    ys, sr_new, si_new = s5_scan(
        u_s.astype(f32).reshape(bsz, L, S5_GROUPS, S5_GROUP), s5_re.astype(f32), s5_im.astype(f32),
        ab_re, ab_im, bb_re, bb_im, p['s5_c_re'].astype(f32), p['s5_c_im'].astype(f32), p['s5_d'].astype(f32))
    g = jax.nn.gelu(ys)
    ys = (g * jax.nn.sigmoid(g @ p['s5_w_glu'].astype(f32) + p['s5_b_glu'].astype(f32))).astype(dt)
    mix = jnp.concatenate([hm, ys], axis=-1) @ p['w_out']
    x = x + rmsnorm(mix, p['mix_norm_post'])
    h = rmsnorm(x, p['ff2_norm_pre'])
    x = x + 0.5 * rmsnorm(swiglu(h, p['ff2_w_gate'], p['ff2_w_up'], p['ff2_w_down']), p['ff2_norm_post'])
    return x, c_new, n_new, m_new, conv_new, sr_new, si_new


def setup_inputs(seed: int = 0) -> dict:
    key = jax.random.key(seed)
    ks = jax.random.split(key, 40)
    f32 = jnp.float32

    def nrm(k, shape, scale):
        return jax.random.normal(k, shape, f32) * scale

    def gain(k, shape):
        return 1.0 + 0.01 * jax.random.normal(k, shape, f32)

    b_f = jnp.linspace(3.0, 6.0, M_HEADS, dtype=f32)[None, :] + nrm(ks[20], (DEPTH, M_HEADS), 0.1)
    a_im = jnp.pi * jnp.arange(S5_STATE, dtype=f32)[None, None, :] + nrm(ks[23], (DEPTH, S5_GROUPS, S5_STATE), 0.01)
    return {
        'x_prompt': nrm(ks[0], (BATCH, SEQ, D_MODEL), 1.0),
        'x_sample': nrm(ks[1], (DEC_BATCH, DEC_SEQ, D_MODEL), 1.0),
        'state_mlstm_C': nrm(ks[2], (DEPTH, DEC_BATCH, M_HEADS, M_DQK, M_DV), M_DQK ** -0.5),
        'state_mlstm_n': nrm(ks[3], (DEPTH, DEC_BATCH, M_HEADS, M_DQK), 1.0),
        'state_mlstm_m': nrm(ks[4], (DEPTH, DEC_BATCH, M_HEADS), 1.0),
        'cache_mlstm_conv': nrm(ks[5], (DEPTH, DEC_BATCH, CONV_W - 1, D_MLSTM), 1.0),
        'state_s5_re': nrm(ks[6], (DEPTH, DEC_BATCH, S5_GROUPS, S5_STATE), 0.5),
        'state_s5_im': nrm(ks[7], (DEPTH, DEC_BATCH, S5_GROUPS, S5_STATE), 0.5),
        'ff1_norm_pre': gain(ks[8], (DEPTH, D_MODEL)),
        'ff1_norm_post': gain(ks[9], (DEPTH, D_MODEL)),
        'ff1_w_gate': nrm(ks[10], (DEPTH, D_MODEL, D_FF), D_MODEL ** -0.5),
        'ff1_w_up': nrm(ks[11], (DEPTH, D_MODEL, D_FF), D_MODEL ** -0.5),
        'ff1_w_down': nrm(ks[12], (DEPTH, D_FF, D_MODEL), D_FF ** -0.5),
        'mix_norm_pre': gain(ks[13], (DEPTH, D_MODEL)),
        'w_in': nrm(ks[14], (DEPTH, D_MODEL, D_IN), D_MODEL ** -0.5),
        'mlstm_conv_w': nrm(ks[15], (DEPTH, CONV_W, D_MLSTM), CONV_W ** -0.5),
        'mlstm_conv_b': nrm(ks[16], (DEPTH, D_MLSTM), 0.01),
        'mlstm_w_q': nrm(ks[17], (DEPTH, M_HEADS, M_DV, M_DQK), M_DV ** -0.5),
        'mlstm_w_k': nrm(ks[18], (DEPTH, M_HEADS, M_DV, M_DQK), M_DV ** -0.5),
        'mlstm_b_i': nrm(ks[19], (DEPTH, M_HEADS), 0.1),
        'mlstm_b_f': b_f,
        'mlstm_head_norm': gain(ks[21], (DEPTH, M_HEADS, M_DV)),
        's5_a_re': -0.5 + nrm(ks[22], (DEPTH, S5_GROUPS, S5_STATE), 0.01),
        's5_a_im': a_im,
        's5_log_dt': jax.random.uniform(ks[24], (DEPTH, S5_GROUPS), f32, math.log(1e-3), math.log(1e-1)),
        's5_b_re': nrm(ks[25], (DEPTH, S5_GROUPS, S5_STATE, S5_GROUP), (2 * S5_GROUP) ** -0.5),
        's5_b_im': nrm(ks[26], (DEPTH, S5_GROUPS, S5_STATE, S5_GROUP), (2 * S5_GROUP) ** -0.5),
        's5_c_re': nrm(ks[27], (DEPTH, S5_GROUPS, S5_GROUP, S5_STATE), (2 * S5_STATE) ** -0.5),
        's5_c_im': nrm(ks[28], (DEPTH, S5_GROUPS, S5_GROUP, S5_STATE), (2 * S5_STATE) ** -0.5),
        's5_d': nrm(ks[29], (DEPTH, S5_GROUPS, S5_GROUP), 1.0),
        's5_w_glu': nrm(ks[30], (DEPTH, D_S5, D_S5), D_S5 ** -0.5),
        's5_b_glu': nrm(ks[31], (DEPTH, D_S5), 0.01),
        'w_out': nrm(ks[32], (DEPTH, D_MIX, D_MODEL), D_MIX ** -0.5),
        'mix_norm_post': gain(ks[33], (DEPTH, D_MODEL)),
        'ff2_norm_pre': gain(ks[34], (DEPTH, D_MODEL)),
        'ff2_norm_post': gain(ks[35], (DEPTH, D_MODEL)),
        'ff2_w_gate': nrm(ks[36], (DEPTH, D_MODEL, D_FF), D_MODEL ** -0.5),
        'ff2_w_up': nrm(ks[37], (DEPTH, D_MODEL, D_FF), D_MODEL ** -0.5),
        'ff2_w_down': nrm(ks[38], (DEPTH, D_FF, D_MODEL), D_FF ** -0.5),
    }


def reference(x_prompt, x_sample, state_mlstm_C, state_mlstm_n, state_mlstm_m, cache_mlstm_conv,
              state_s5_re, state_s5_im, ff1_norm_pre, ff1_norm_post, ff1_w_gate, ff1_w_up, ff1_w_down,
              mix_norm_pre, w_in, mlstm_conv_w, mlstm_conv_b, mlstm_w_q, mlstm_w_k, mlstm_b_i, mlstm_b_f,
              mlstm_head_norm, s5_a_re, s5_a_im, s5_log_dt, s5_b_re, s5_b_im, s5_c_re, s5_c_im, s5_d,
              s5_w_glu, s5_b_glu, w_out, mix_norm_post, ff2_norm_pre, ff2_norm_post, ff2_w_gate, ff2_w_up,
              ff2_w_down):
    f32 = jnp.float32
    bp = x_prompt.shape[0]
    yp = x_prompt
    ys = x_sample
    new_p = []
    new_s = []
    for l in range(DEPTH):
        p = dict(ff1_norm_pre=ff1_norm_pre[l], ff1_norm_post=ff1_norm_post[l], ff1_w_gate=ff1_w_gate[l],
                 ff1_w_up=ff1_w_up[l], ff1_w_down=ff1_w_down[l], mix_norm_pre=mix_norm_pre[l], w_in=w_in[l],
                 mlstm_conv_w=mlstm_conv_w[l], mlstm_conv_b=mlstm_conv_b[l], mlstm_w_q=mlstm_w_q[l],
                 mlstm_w_k=mlstm_w_k[l], mlstm_b_i=mlstm_b_i[l], mlstm_b_f=mlstm_b_f[l],
                 mlstm_head_norm=mlstm_head_norm[l], s5_a_re=s5_a_re[l], s5_a_im=s5_a_im[l],
                 s5_log_dt=s5_log_dt[l], s5_b_re=s5_b_re[l], s5_b_im=s5_b_im[l], s5_c_re=s5_c_re[l],
                 s5_c_im=s5_c_im[l], s5_d=s5_d[l], s5_w_glu=s5_w_glu[l], s5_b_glu=s5_b_glu[l],
                 w_out=w_out[l], mix_norm_post=mix_norm_post[l], ff2_norm_pre=ff2_norm_pre[l],
                 ff2_norm_post=ff2_norm_post[l], ff2_w_gate=ff2_w_gate[l], ff2_w_up=ff2_w_up[l],
                 ff2_w_down=ff2_w_down[l])
        yp, pc, pn, pm, pconv, pre, pim = layer(
            yp,
            jnp.zeros((bp, M_HEADS, M_DQK, M_DV), f32),
            jnp.zeros((bp, M_HEADS, M_DQK), f32),
            jnp.zeros((bp, M_HEADS), f32),
            jnp.zeros((bp, CONV_W - 1, D_MLSTM), yp.dtype),
            jnp.zeros((bp, S5_GROUPS, S5_STATE), f32),
            jnp.zeros((bp, S5_GROUPS, S5_STATE), f32),
            p)
        new_p.append((pc, pn, pm, pconv, pre, pim))
        ys, sc, sn, sm, sconv, sre, sim = layer(
            ys, state_mlstm_C[l], state_mlstm_n[l], state_mlstm_m[l], cache_mlstm_conv[l],
            state_s5_re[l], state_s5_im[l], p)
        new_s.append((sc, sn, sm, sconv, sre, sim))
    c_p, n_p, m_p, conv_p, re_p, im_p = [jnp.stack([e[i] for e in new_p]) for i in range(6)]
    c_s, n_s, m_s, conv_s, re_s, im_s = [jnp.stack([e[i] for e in new_s]) for i in range(6)]
    return (yp, ys, c_p, n_p, m_p, conv_p, re_p, im_p, c_s, n_s, m_s, conv_s, re_s, im_s)
```

```python
import functools
import math

import jax
import jax.numpy as jnp
from jax import lax
from jax.experimental import pallas as pl
from jax.experimental.pallas import tpu as pltpu

F32 = jnp.float32
BF16 = jnp.bfloat16
RMS_EPS = 1e-6
CONV_W = 4
S5_WIN = 16
S5_GROUP = 16
S5_STATE = 64
VMEM_LIMIT = 56 * 1024 * 1024
HI = lax.Precision.HIGHEST


def _cparams(n_axes):
    return pltpu.CompilerParams(dimension_semantics=("arbitrary",) * n_axes,
                                vmem_limit_bytes=VMEM_LIMIT)


def _rms(v, g):
    return v * lax.rsqrt(jnp.mean(v * v, axis=-1, keepdims=True) + RMS_EPS) * g


def _resnorm_kernel(*refs, alpha, has_y, want_x, want_n):
    refs = list(refs)
    x_ref = refs.pop(0)
    x = x_ref[...]
    if has_y:
        y_ref = refs.pop(0)
        gpost_ref = refs.pop(0)
        x = x + alpha * _rms(y_ref[...], gpost_ref[...])
    if want_n:
        gpre_ref = refs.pop(0)
    if want_x:
        refs.pop(0)[...] = x
    if want_n:
        refs.pop(0)[...] = _rms(x, gpre_ref[...]).astype(BF16)


def _resnorm(x, y=None, g_post=None, g_pre=None, *, alpha=1.0, want_x=True, tr=256):
    rows, d = x.shape
    has_y = y is not None
    want_n = g_pre is not None
    row_spec = pl.BlockSpec((tr, d), lambda i: (i, 0))
    vec_spec = pl.BlockSpec((1, d), lambda i: (0, 0))
    args, in_specs = [x], [row_spec]
    if has_y:
        args += [y, g_post.reshape(1, d)]
        in_specs += [row_spec, vec_spec]
    if want_n:
        args.append(g_pre.reshape(1, d))
        in_specs.append(vec_spec)
    out_shape, out_specs = [], []
    if want_x:
        out_shape.append(jax.ShapeDtypeStruct((rows, d), F32))
        out_specs.append(row_spec)
    if want_n:
        out_shape.append(jax.ShapeDtypeStruct((rows, d), BF16))
        out_specs.append(row_spec)
    return pl.pallas_call(
        functools.partial(_resnorm_kernel, alpha=alpha, has_y=has_y, want_x=want_x, want_n=want_n),
        grid=(rows // tr,), in_specs=in_specs, out_specs=out_specs, out_shape=out_shape,
        compiler_params=_cparams(1), name="resnorm")(*args)


def _ffn_kernel(xn_ref, wg_ref, wu_ref, wd_ref, o_ref):
    j = pl.program_id(1)
    xn = xn_ref[...]
    g = jnp.dot(xn, wg_ref[...], preferred_element_type=F32)
    u = jnp.dot(xn, wu_ref[...], preferred_element_type=F32)
    h = (g * jax.nn.sigmoid(g) * u).astype(BF16)
    d = jnp.dot(h, wd_ref[...], preferred_element_type=F32)

    @pl.when(j == 0)
    def _():
        o_ref[...] = d

    @pl.when(j > 0)
    def _():
        o_ref[...] += d


def _ffn(xn, wg, wu, wd, *, tm=768, tf=256):
    rows, d = xn.shape
    f = wg.shape[1]
    return pl.pallas_call(
        _ffn_kernel,
        grid=(rows // tm, f // tf),
        in_specs=[pl.BlockSpec((tm, d), lambda i, j: (i, 0), pipeline_mode=pl.Buffered(1)),
                  pl.BlockSpec((d, tf), lambda i, j: (0, j)),
                  pl.BlockSpec((d, tf), lambda i, j: (0, j)),
                  pl.BlockSpec((tf, d), lambda i, j: (j, 0))],
        out_specs=pl.BlockSpec((tm, d), lambda i, j: (i, 0)),
        out_shape=jax.ShapeDtypeStruct((rows, d), F32),
        compiler_params=_cparams(2), name="ffn")(xn, wg, wu, wd)


def _mm_kernel(*refs, n_pairs):
    o_ref = refs[-1]
    acc = None
    for p in range(n_pairs):
        d = jnp.dot(refs[2 * p][...], refs[2 * p + 1][...], preferred_element_type=F32)
        acc = d if acc is None else acc + d
    o_ref[...] = acc.astype(o_ref.dtype)


def _mm(pairs, *, tm=768, tn=512, out_dtype=F32):
    rows = pairs[0][0].shape[0]
    n = pairs[0][1].shape[1]
    tn = min(tn, n)
    args, in_specs = [], []
    for a, w in pairs:
        k = a.shape[1]
        args += [a, w]
        in_specs += [pl.BlockSpec((tm, k), lambda i, j: (i, 0)),
                     pl.BlockSpec((k, tn), lambda i, j: (0, j))]
    return pl.pallas_call(
        functools.partial(_mm_kernel, n_pairs=len(pairs)),
        grid=(rows // tm, n // tn),
        in_specs=in_specs,
        out_specs=pl.BlockSpec((tm, tn), lambda i, j: (i, j)),
        out_shape=jax.ShapeDtypeStruct((rows, n), out_dtype),
        compiler_params=_cparams(2), name="mm")(*args)


def _cumsum_rows(x):
    t = x.shape[0]
    row = lax.broadcasted_iota(jnp.int32, x.shape, 0)
    k = 1
    while k < t:
        x = x + jnp.where(row >= k, pltpu.roll(x, k, axis=0), 0.0)
        k *= 2
    return x


def _mlstm_kernel(um_ref, v_ref, o_ref, gt_ref, cbuf_ref, c0_ref, n0_ref, m0_ref,
                  convw_ref, convb_ref, wq_ref, wk_ref, gbias_ref, hn_ref,
                  hm_ref, cout_ref, nout_ref, mout_ref,
                  xp_s, c_s, n_s, m_s, *, t, heads, dqk, dv, n_chunks):
    c = pl.program_id(1)

    @pl.when(c == 0)
    def _():
        xp_s[0:8, :] = cbuf_ref[0]
        c_s[...] = c0_ref[0]
        n_s[...] = n0_ref[0]
        m_s[...] = m0_ref[0]

    xp_s[8:8 + t, :] = um_ref[...]
    cv = convb_ref[...] + xp_s[8 - (CONV_W - 1):8 - (CONV_W - 1) + t, :] * convw_ref[0:1, :]
    for j in range(1, CONV_W):
        off = 8 - (CONV_W - 1) + j
        cv = cv + xp_s[off:off + t, :] * convw_ref[j:j + 1, :]
    cv = cv * jax.nn.sigmoid(cv)
    xp_s[0:8, :] = xp_s[t:t + 8, :]

    gt = gt_ref[...] + gbias_ref[...]
    lane = lax.broadcasted_iota(jnp.int32, gt.shape, 1)
    lf = jnp.minimum(gt, 0.0) - jnp.log1p(jnp.exp(-jnp.abs(gt)))
    g_all = jnp.where(lane < heads, gt, lf)
    b_all = _cumsum_rows(g_all)
    mix = jnp.where(lane < heads, g_all, b_all)
    tp = max(t, 128)
    if tp != t:
        mix_p = jnp.concatenate([mix, jnp.zeros((tp - t, mix.shape[1]), F32)], axis=0)
    else:
        mix_p = mix
    mix_t = mix_p.T[:, :t]

    rr = lax.broadcasted_iota(jnp.int32, (t, t), 0)
    cc = lax.broadcasted_iota(jnp.int32, (t, t), 1)
    causal = cc <= rr

    for h in range(heads):
        ig_c = mix[:, h:h + 1]
        b_c = mix[:, heads + h:heads + h + 1]
        ig_r = mix_t[h:h + 1, :]
        b_r = mix_t[heads + h:heads + h + 1, :]
        m0 = m_s[h:h + 1, 0:1]
        log_d = jnp.where(causal, b_c - b_r + ig_r, -jnp.inf)
        inter = b_c + m0
        m_t = jnp.maximum(inter, jnp.max(log_d, axis=1, keepdims=True))
        dmat = jnp.exp(log_d - m_t)
        cvh = cv[:, h * dv:(h + 1) * dv].astype(BF16)
        q = jnp.dot(cvh, wq_ref[h], preferred_element_type=F32)
        k = jnp.dot(cvh, wk_ref[h], preferred_element_type=F32) * (dqk ** -0.5)
        qb = q.astype(BF16)
        kb = k.astype(BF16)
        s = lax.dot_general(qb, kb, (((1,), (1,)), ((), ())), preferred_element_type=F32) * dmat
        w_inter = jnp.exp(inter - m_t)
        c0 = c_s[h]
        n0 = n_s[h:h + 1, :]
        vh = v_ref[:, h * dv:(h + 1) * dv]
        num = (w_inter * jnp.dot(qb, c0.astype(BF16), preferred_element_type=F32)
               + jnp.dot(s.astype(BF16), vh.astype(BF16), preferred_element_type=F32))
        den = w_inter * jnp.sum(q * n0, axis=1, keepdims=True) + jnp.sum(s, axis=1, keepdims=True)
        hh = num / jnp.maximum(jnp.abs(den), jnp.exp(-m_t))
        b_last = b_c[t - 1:t, :]
        log_w = b_last - b_c + ig_c
        m_new = jnp.maximum(b_last + m0, jnp.max(log_w, axis=0, keepdims=True))
        w_col = jnp.exp(log_w - m_new)
        decay = jnp.exp(b_last + m0 - m_new)
        wv = (w_col * vh).astype(BF16)
        c_s[h] = decay * c0 + lax.dot_general(kb, wv, (((0,), (0,)), ((), ())),
                                              preferred_element_type=F32)
        n_s[h:h + 1, :] = decay * n0 + jnp.sum(w_col * k, axis=0, keepdims=True)
        m_s[h:h + 1, :] = jnp.broadcast_to(m_new, (1, m_s.shape[1]))
        hn = _rms(hh, hn_ref[h:h + 1, :])
        og = jax.nn.sigmoid(o_ref[:, h * dv:(h + 1) * dv])
        hm_ref[:, h * dv:(h + 1) * dv] = (hn * og).astype(BF16)

    @pl.when(c == n_chunks - 1)
    def _():
        cout_ref[0] = c_s[...]
        nout_ref[0] = n_s[...]
        mout_ref[0] = m_s[...]


def _mlstm(p, gates, row0, n_streams, n_chunks, t, cbuf, c0, n0, m0,
           convw, convb, wq, wk, gbias, hnorm):
    heads, dv, dqk = wq.shape
    dm = heads * dv
    blk0 = row0 // t
    assert row0 % t == 0

    def rows(b, c):
        return blk0 + b * n_chunks + c

    kern = functools.partial(_mlstm_kernel, t=t, heads=heads, dqk=dqk, dv=dv, n_chunks=n_chunks)
    n_rows = n_streams * n_chunks * t
    outs = pl.pallas_call(
        kern,
        grid=(n_streams, n_chunks),
        in_specs=[pl.BlockSpec((t, dm), lambda b, c: (rows(b, c), 0)),
                  pl.BlockSpec((t, dm), lambda b, c: (rows(b, c), 1)),
                  pl.BlockSpec((t, dm), lambda b, c: (rows(b, c), 2)),
                  pl.BlockSpec((t, 128), lambda b, c: (rows(b, c), 0)),
                  pl.BlockSpec((1, 8, dm), lambda b, c: (b, 0, 0)),
                  pl.BlockSpec((1, heads, dqk, dv), lambda b, c: (b, 0, 0, 0)),
                  pl.BlockSpec((1, 8, dqk), lambda b, c: (b, 0, 0)),
                  pl.BlockSpec((1, 8, 128), lambda b, c: (b, 0, 0)),
                  pl.BlockSpec((CONV_W, dm), lambda b, c: (0, 0)),
                  pl.BlockSpec((1, dm), lambda b, c: (0, 0)),
                  pl.BlockSpec((heads, dv, dqk), lambda b, c: (0, 0, 0)),
                  pl.BlockSpec((heads, dv, dqk), lambda b, c: (0, 0, 0)),
                  pl.BlockSpec((1, 128), lambda b, c: (0, 0)),
                  pl.BlockSpec((heads, dv), lambda b, c: (0, 0))],
        out_specs=[pl.BlockSpec((t, dm), lambda b, c: (b * n_chunks + c, 0)),
                   pl.BlockSpec((1, heads, dqk, dv), lambda b, c: (b, 0, 0, 0)),
                   pl.BlockSpec((1, 8, dqk), lambda b, c: (b, 0, 0)),
                   pl.BlockSpec((1, 8, 128), lambda b, c: (b, 0, 0))],
        out_shape=[jax.ShapeDtypeStruct((n_rows, dm), BF16),
                   jax.ShapeDtypeStruct((n_streams, heads, dqk, dv), F32),
                   jax.ShapeDtypeStruct((n_streams, 8, dqk), F32),
                   jax.ShapeDtypeStruct((n_streams, 8, 128), F32)],
        scratch_shapes=[pltpu.VMEM((t + 8, dm), F32),
                        pltpu.VMEM((heads, dqk, dv), F32),
                        pltpu.VMEM((8, dqk), F32),
                        pltpu.VMEM((8, 128), F32)],
        compiler_params=_cparams(2), name="mlstm")(
            p, p, p, gates, cbuf, c0, n0, m0, convw, convb, wq, wk, gbias, hnorm)
    return outs


def _s5_prep_kernel(ac_re_ref, ac_im_ref, ar_re_ref, ar_im_ref, ldt_ref, bt_re_ref, bt_im_ref,
                    ct_re_ref, ct_im_ref, d_ref,
                    k_ref, bm_re_ref, bm_im_ref, cm_re_ref, cm_im_ref, a16_re_ref, a16_im_ref):
    win, grp, st = S5_WIN, S5_GROUP, S5_STATE
    wl = win * grp
    lane_w = lax.broadcasted_iota(jnp.int32, (1, wl), 1)
    tau = (lane_w // grp).astype(F32)
    sel = (lax.broadcasted_iota(jnp.int32, (grp, wl), 1) % grp
           == lax.broadcasted_iota(jnp.int32, (grp, wl), 0)).astype(F32)
    row_w = lax.broadcasted_iota(jnp.int32, (wl, 1), 0)
    srev = (win - 1 - row_w // grp).astype(F32)
    lane_p = lax.broadcasted_iota(jnp.int32, (1, 2 * st), 1)
    diag = (lax.broadcasted_iota(jnp.int32, (grp, wl), 1)
            == lax.broadcasted_iota(jnp.int32, (grp, wl), 0))

    a16_re = jnp.zeros((1, 2 * st), F32)
    a16_im = jnp.zeros((1, 2 * st), F32)
    cm_re_parts, cm_im_parts = [], []
    for e in range(2):
        dt = jnp.exp(ldt_ref[0, e])
        arg_re = ac_re_ref[0, e] * dt
        arg_im = ac_im_ref[0, e] * dt
        mag = jnp.exp(arg_re * tau)
        pw_re = mag * jnp.cos(arg_im * tau)
        pw_im = mag * jnp.sin(arg_im * tau)
        cx_re = jnp.dot(ct_re_ref[0, e], sel, precision=HI, preferred_element_type=F32)
        cx_im = jnp.dot(ct_im_ref[0, e], sel, precision=HI, preferred_element_type=F32)
        q_re = cx_re * pw_re - cx_im * pw_im
        q_im = cx_re * pw_im + cx_im * pw_re
        a1_re = jnp.exp(arg_re) * jnp.cos(arg_im)
        a1_im = jnp.exp(arg_re) * jnp.sin(arg_im)
        cm_re = q_re * a1_re - q_im * a1_im
        cm_im = q_re * a1_im + q_im * a1_re
        zero = jnp.zeros_like(cm_re)
        cm_re_parts.append(jnp.concatenate([cm_re, zero] if e == 0 else [zero, cm_re], axis=1))
        cm_im_parts.append(jnp.concatenate([-cm_im, zero] if e == 0 else [zero, -cm_im], axis=1))

        ar = ar_re_ref[0, e]
        ai = ar_im_ref[0, e]
        ab_re = jnp.exp(ar * dt) * jnp.cos(ai * dt)
        ab_im = jnp.exp(ar * dt) * jnp.sin(ai * dt)
        den = ar * ar + ai * ai
        inv_re = ar / den
        inv_im = -ai / den
        f_re = ab_re - 1.0
        f_im = ab_im
        z_re = f_re * inv_re - f_im * inv_im
        z_im = f_re * inv_im + f_im * inv_re
        bb_re = z_re * bt_re_ref[0, e] - z_im * bt_im_ref[0, e]
        bb_im = z_re * bt_im_ref[0, e] + z_im * bt_re_ref[0, e]

        w = (jnp.dot(bb_re[:, :st], q_re, precision=HI, preferred_element_type=F32)
             - jnp.dot(bb_im[:, :st], q_im, precision=HI, preferred_element_type=F32))
        dx = jnp.dot(d_ref[0, e], sel, precision=HI, preferred_element_type=F32)
        w = w + jnp.where(diag, dx, 0.0)
        lane_k = lax.broadcasted_iota(jnp.int32, (grp, wl), 1)
        blocks = [w]
        for s in range(1, win):
            blocks.append(jnp.where(lane_k >= grp * s, pltpu.roll(w, grp * s, axis=1), 0.0))
        k_ref[e] = jnp.concatenate(blocks, axis=0).astype(BF16)

        magb = jnp.exp(ar * dt * srev)
        pb_re = magb * jnp.cos(ai * dt * srev)
        pb_im = magb * jnp.sin(ai * dt * srev)
        tb_re = jnp.concatenate([bb_re] * win, axis=0)
        tb_im = jnp.concatenate([bb_im] * win, axis=0)
        half = (lane_p // st) == e
        bm_re_ref[0, e * wl:(e + 1) * wl, :] = jnp.where(half, tb_re * pb_re - tb_im * pb_im, 0.0).astype(BF16)
        bm_im_ref[0, e * wl:(e + 1) * wl, :] = jnp.where(half, tb_re * pb_im + tb_im * pb_re, 0.0).astype(BF16)

        m16 = jnp.exp(ar * dt * float(win))
        a16_re = jnp.where(half, m16 * jnp.cos(ai * dt * float(win)), a16_re)
        a16_im = jnp.where(half, m16 * jnp.sin(ai * dt * float(win)), a16_im)

    cm_re_ref[0] = jnp.concatenate(cm_re_parts, axis=0).astype(BF16)
    cm_im_ref[0] = jnp.concatenate(cm_im_parts, axis=0).astype(BF16)
    a16_re_ref[0] = a16_re
    a16_im_ref[0] = a16_im


def _s5_prep(a_re, a_im, log_dt, b_re, b_im, c_re, c_im, d):
    g, st = a_re.shape
    grp, win = S5_GROUP, S5_WIN
    wl = win * grp
    npair = g // 2

    def pairs(x):
        return x.reshape((npair, 2) + x.shape[1:])

    dup = lambda x: jnp.concatenate([x, x], axis=-1)
    args = [pairs(a_re[:, :, None]), pairs(a_im[:, :, None]),
            pairs(dup(a_re)[:, None, :]), pairs(dup(a_im)[:, None, :]),
            pairs(log_dt[:, None, None]),
            pairs(dup(jnp.swapaxes(b_re, 1, 2))), pairs(dup(jnp.swapaxes(b_im, 1, 2))),
            pairs(jnp.swapaxes(c_re, 1, 2)), pairs(jnp.swapaxes(c_im, 1, 2)),
            pairs(d[:, None, :])]

    def spec(x):
        nd = x.ndim
        return pl.BlockSpec((1,) + x.shape[1:], lambda k: (k,) + (0,) * (nd - 1))

    out_shape = [jax.ShapeDtypeStruct((g, wl, wl), BF16),
                 jax.ShapeDtypeStruct((npair, 2 * wl, 2 * st), BF16),
                 jax.ShapeDtypeStruct((npair, 2 * wl, 2 * st), BF16),
                 jax.ShapeDtypeStruct((npair, 2 * st, 2 * wl), BF16),
                 jax.ShapeDtypeStruct((npair, 2 * st, 2 * wl), BF16),
                 jax.ShapeDtypeStruct((npair, 1, 2 * st), F32),
                 jax.ShapeDtypeStruct((npair, 1, 2 * st), F32)]
    out_specs = [pl.BlockSpec((2, wl, wl), lambda k: (k, 0, 0)),
                 pl.BlockSpec((1, 2 * wl, 2 * st), lambda k: (k, 0, 0)),
                 pl.BlockSpec((1, 2 * wl, 2 * st), lambda k: (k, 0, 0)),
                 pl.BlockSpec((1, 2 * st, 2 * wl), lambda k: (k, 0, 0)),
                 pl.BlockSpec((1, 2 * st, 2 * wl), lambda k: (k, 0, 0)),
                 pl.BlockSpec((1, 1, 2 * st), lambda k: (k, 0, 0)),
                 pl.BlockSpec((1, 1, 2 * st), lambda k: (k, 0, 0))]
    return pl.pallas_call(
        _s5_prep_kernel, grid=(npair,),
        in_specs=[spec(x) for x in args], out_specs=out_specs, out_shape=out_shape,
        compiler_params=_cparams(1), name="s5_prep")(*args)


def _s5_kernel(u_ref, k_ref, bm_re_ref, bm_im_ref, cm_re_ref, cm_im_ref, a16_re_ref, a16_im_ref,
               h0_re_ref, h0_im_ref,
               y_ref, hp_re_ref, hp_im_ref, hs_re_ref, hs_im_ref,
               h_re_s, h_im_s, *, n_prompt, pb):
    ln = 2 * S5_STATE
    wl = S5_WIN * S5_GROUP
    n_rows = u_ref.shape[1]
    for k in range(pb):
        u = u_ref[k]
        h_re_s[:, k * ln:(k + 1) * ln] = jnp.dot(u, bm_re_ref[k], preferred_element_type=F32)
        h_im_s[:, k * ln:(k + 1) * ln] = jnp.dot(u, bm_im_ref[k], preferred_element_type=F32)
    a_re = a16_re_ref[0]
    a_im = a16_im_ref[0]

    h0r = h0_re_ref[...]
    h0i = h0_im_ref[...]
    hs_re_ref[...] = a_re * h0r - a_im * h0i + h_re_s[n_prompt:n_rows, :]
    hs_im_ref[...] = a_re * h0i + a_im * h0r + h_im_s[n_prompt:n_rows, :]
    h_re_s[n_prompt:n_rows, :] = h0r
    h_im_s[n_prompt:n_rows, :] = h0i

    def body(c, carry):
        hr, hi = carry
        lr = h_re_s[pl.ds(c, 1), :]
        li = h_im_s[pl.ds(c, 1), :]
        h_re_s[pl.ds(c, 1), :] = hr
        h_im_s[pl.ds(c, 1), :] = hi
        return a_re * hr - a_im * hi + lr, a_re * hi + a_im * hr + li

    zero = jnp.zeros((1, pb * ln), F32)
    hr, hi = lax.fori_loop(0, n_prompt, body, (zero, zero))
    hp_re_ref[...] = jnp.broadcast_to(hr, hp_re_ref.shape)
    hp_im_ref[...] = jnp.broadcast_to(hi, hp_im_ref.shape)

    for k in range(pb):
        u = u_ref[k]
        hs_r = h_re_s[:, k * ln:(k + 1) * ln].astype(BF16)
        hs_i = h_im_s[:, k * ln:(k + 1) * ln].astype(BF16)
        y = jnp.concatenate(
            [jnp.dot(u[:, :wl], k_ref[2 * k], preferred_element_type=F32),
             jnp.dot(u[:, wl:], k_ref[2 * k + 1], preferred_element_type=F32)], axis=1)
        y = y + jnp.dot(hs_r, cm_re_ref[k], preferred_element_type=F32)
        y = y + jnp.dot(hs_i, cm_im_ref[k], preferred_element_type=F32)
        y_ref[k] = y


def _s5(u2, kmat, bm_re, bm_im, cm_re, cm_im, a16_re, a16_im, h0_re, h0_im, *, n_prompt, pb=8):
    npair, n_rows, w2 = u2.shape
    n_s = n_rows - n_prompt
    ln = 2 * S5_STATE
    lanes = npair * ln
    steps = npair // pb
    a16_re = a16_re.reshape(steps, 1, pb * ln)
    a16_im = a16_im.reshape(steps, 1, pb * ln)
    kern = functools.partial(_s5_kernel, n_prompt=n_prompt, pb=pb)
    return pl.pallas_call(
        kern, grid=(steps,),
        in_specs=[pl.BlockSpec((pb, n_rows, w2), lambda i: (i, 0, 0)),
                  pl.BlockSpec((2 * pb, w2 // 2, w2 // 2), lambda i: (i, 0, 0)),
                  pl.BlockSpec((pb, w2, ln), lambda i: (i, 0, 0)),
                  pl.BlockSpec((pb, w2, ln), lambda i: (i, 0, 0)),
                  pl.BlockSpec((pb, ln, w2), lambda i: (i, 0, 0)),
                  pl.BlockSpec((pb, ln, w2), lambda i: (i, 0, 0)),
                  pl.BlockSpec((1, 1, pb * ln), lambda i: (i, 0, 0)),
                  pl.BlockSpec((1, 1, pb * ln), lambda i: (i, 0, 0)),
                  pl.BlockSpec((n_s, pb * ln), lambda i: (0, i)),
                  pl.BlockSpec((n_s, pb * ln), lambda i: (0, i))],
        out_specs=[pl.BlockSpec((pb, n_rows, w2), lambda i: (i, 0, 0)),
                   pl.BlockSpec((8, pb * ln), lambda i: (0, i)),
                   pl.BlockSpec((8, pb * ln), lambda i: (0, i)),
                   pl.BlockSpec((n_s, pb * ln), lambda i: (0, i)),
                   pl.BlockSpec((n_s, pb * ln), lambda i: (0, i))],
        out_shape=[jax.ShapeDtypeStruct((npair, n_rows, w2), F32),
                   jax.ShapeDtypeStruct((8, lanes), F32),
                   jax.ShapeDtypeStruct((8, lanes), F32),
                   jax.ShapeDtypeStruct((n_s, lanes), F32),
                   jax.ShapeDtypeStruct((n_s, lanes), F32)],
        scratch_shapes=[pltpu.VMEM((n_rows, pb * ln), F32), pltpu.VMEM((n_rows, pb * ln), F32)],
        compiler_params=_cparams(1), name="s5")(
            u2, kmat, bm_re, bm_im, cm_re, cm_im, a16_re, a16_im, h0_re, h0_im)


def _glu_kernel(y_ref, w_ref, b_ref, o_ref):
    g = jax.nn.gelu(y_ref[...])
    z = jnp.dot(g.astype(BF16), w_ref[...], preferred_element_type=F32) + b_ref[...]
    o_ref[...] = (g * jax.nn.sigmoid(z)).astype(BF16)


def _glu(y, w, b, *, tm=768):
    rows, d = y.shape
    return pl.pallas_call(
        _glu_kernel, grid=(rows // tm,),
        in_specs=[pl.BlockSpec((tm, d), lambda i: (i, 0)),
                  pl.BlockSpec((d, d), lambda i: (0, 0)),
                  pl.BlockSpec((1, d), lambda i: (0, 0))],
        out_specs=pl.BlockSpec((tm, d), lambda i: (i, 0)),
        out_shape=jax.ShapeDtypeStruct((rows, d), BF16),
        compiler_params=_cparams(1), name="glu")(y, w, b.reshape(1, d))


def _pad_rows8(x, at_end=False):
    b, r, n = x.shape
    z = jnp.zeros((b, 8 - r, n), x.dtype)
    return jnp.concatenate([x, z] if not at_end else [z, x], axis=1)


def kernel(x_prompt, x_sample, state_mlstm_C, state_mlstm_n, state_mlstm_m, cache_mlstm_conv,
           state_s5_re, state_s5_im, ff1_norm_pre, ff1_norm_post, ff1_w_gate, ff1_w_up, ff1_w_down,
           mix_norm_pre, w_in, mlstm_conv_w, mlstm_conv_b, mlstm_w_q, mlstm_w_k, mlstm_b_i, mlstm_b_f,
           mlstm_head_norm, s5_a_re, s5_a_im, s5_log_dt, s5_b_re, s5_b_im, s5_c_re, s5_c_im, s5_d,
           s5_w_glu, s5_b_glu, w_out, mix_norm_post, ff2_norm_pre, ff2_norm_post, ff2_w_gate, ff2_w_up,
           ff2_w_down):
    depth = w_in.shape[0]
    bp, lp, d = x_prompt.shape
    bs, ls, _ = x_sample.shape
    assert bp == 1 and ls == S5_WIN and lp % 256 == 0
    heads, dv, dqk = mlstm_w_q.shape[1:]
    dm = heads * dv
    groups, st = s5_a_re.shape[1:]
    ds5 = groups * S5_GROUP
    n_p = bp * lp
    n_s = bs * ls
    t_prompt = 256

    x = jnp.concatenate([x_prompt.reshape(n_p, d), x_sample.reshape(n_s, d)], axis=0)
    new_p, new_s = [], []
    for l in range(depth):
        bf = lambda w: w[l].astype(BF16)
        xn = _resnorm(x, g_pre=ff1_norm_pre[l], want_x=False)[0]
        y = _ffn(xn, bf(ff1_w_gate), bf(ff1_w_up), bf(ff1_w_down))
        x, hmix = _resnorm(x, y, ff1_norm_post[l], mix_norm_pre[l], alpha=0.5)

        w_in_l = w_in[l]
        n_main = 3 * dm + ds5
        proj = _mm([(hmix, w_in_l[:, :n_main].astype(BF16))])
        w_gate_cols = jnp.pad(w_in_l[:, n_main:], ((0, 0), (0, 128 - 2 * heads))).astype(BF16)
        gates = _mm([(hmix, w_gate_cols)])
        gbias = jnp.pad(jnp.concatenate([mlstm_b_i[l], mlstm_b_f[l]]), (0, 128 - 2 * heads)).reshape(1, 128)

        wq = bf(mlstm_w_q)
        wk = bf(mlstm_w_k)
        convw = mlstm_conv_w[l]
        convb = mlstm_conv_b[l].reshape(1, dm)
        hnorm = mlstm_head_norm[l]
        zc = jnp.zeros((bp, heads, dqk, dv), F32)
        hm_p, c_p, nn_p, m_p = _mlstm(
            proj, gates, 0, bp, lp // t_prompt, t_prompt,
            jnp.zeros((bp, 8, dm), F32), zc, jnp.zeros((bp, 8, dqk), F32), jnp.zeros((bp, 8, 128), F32),
            convw, convb, wq, wk, gbias, hnorm)
        m0_s = jnp.broadcast_to(_pad_rows8(state_mlstm_m[l][:, :, None]), (bs, 8, 128))
        hm_s, c_s, nn_s, m_s = _mlstm(
            proj, gates, n_p, bs, 1, ls,
            _pad_rows8(cache_mlstm_conv[l], at_end=True), state_mlstm_C[l],
            _pad_rows8(state_mlstm_n[l]), m0_s, convw, convb, wq, wk, gbias, hnorm)
        hm = jnp.concatenate([hm_p, hm_s], axis=0)
        um = proj[:, :dm]
        conv_p = um[:n_p].reshape(bp, lp, dm)[:, lp - (CONV_W - 1):]
        conv_s = um[n_p:].reshape(bs, ls, dm)[:, ls - (CONV_W - 1):]

        kmat, bm_re, bm_im, cm_re, cm_im, a16_re, a16_im = _s5_prep(
            s5_a_re[l], s5_a_im[l], s5_log_dt[l], s5_b_re[l], s5_b_im[l], s5_c_re[l], s5_c_im[l], s5_d[l])
        n_win = (n_p + n_s) // S5_WIN
        npair = groups // 2
        u2 = proj[:, 3 * dm:].astype(BF16).reshape(n_win, S5_WIN, npair, 2, S5_GROUP)
        u2 = u2.transpose(2, 0, 3, 1, 4).reshape(npair, n_win, 2 * S5_WIN * S5_GROUP)
        y2, hp_re, hp_im, hs_re, hs_im = _s5(
            u2, kmat, bm_re, bm_im, cm_re, cm_im, a16_re, a16_im,
            state_s5_re[l].reshape(bs, groups * st), state_s5_im[l].reshape(bs, groups * st),
            n_prompt=n_p // S5_WIN)
        ys = y2.reshape(npair, n_win, 2, S5_WIN, S5_GROUP).transpose(1, 3, 0, 2, 4).reshape(n_p + n_s, ds5)
        ys = _glu(ys, bf(s5_w_glu), s5_b_glu[l])

        w_out_l = bf(w_out)
        mix = _mm([(hm, w_out_l[:dm]), (ys, w_out_l[dm:])])
        x, xn = _resnorm(x, mix, mix_norm_post[l], ff2_norm_pre[l], alpha=1.0)
        y = _ffn(xn, bf(ff2_w_gate), bf(ff2_w_up), bf(ff2_w_down))
        x = _resnorm(x, y, ff2_norm_post[l], alpha=0.5)[0]

        new_p.append((c_p, nn_p[:, :heads], m_p[:, :heads, 0], conv_p,
                      hp_re[:1].reshape(bp, groups, st), hp_im[:1].reshape(bp, groups, st)))
        new_s.append((c_s, nn_s[:, :heads], m_s[:, :heads, 0], conv_s,
                      hs_re.reshape(bs, groups, st), hs_im.reshape(bs, groups, st)))

    outs_p = [jnp.stack([e[i] for e in new_p]) for i in range(6)]
    outs_s = [jnp.stack([e[i] for e in new_s]) for i in range(6)]
    y_prompt = x[:n_p].reshape(bp, lp, d)
    y_sample = x[n_p:].reshape(bs, ls, d)
    return (y_prompt, y_sample, *outs_p, *outs_s)
```

```python
import functools

import jax
import jax.numpy as jnp
from jax import lax
from jax.experimental import pallas as pl
from jax.experimental.pallas import tpu as pltpu

F32 = jnp.float32
BF16 = jnp.bfloat16
RMS_EPS = 1e-6
CONV_W = 4
S5_WIN = 16
S5_GROUP = 16
S5_STATE = 64
S5_TILE = 128
VMEM_LIMIT = 56 * 1024 * 1024


def _cparams(n_axes):
    return pltpu.CompilerParams(dimension_semantics=("arbitrary",) * n_axes,
                                vmem_limit_bytes=VMEM_LIMIT)


def _rms(v, g):
    return v * lax.rsqrt(jnp.mean(v * v, axis=-1, keepdims=True) + RMS_EPS) * g


def _resnorm_kernel(*refs, alpha, has_y, want_x, want_n, two_in, two_out, n_ptiles):
    refs = list(refs)
    i = pl.program_id(0)
    if two_in:
        xp_ref = refs.pop(0)
        xs_ref = refs.pop(0)
        x = jnp.where(i < n_ptiles, xp_ref[...], xs_ref[...])
    else:
        x = refs.pop(0)[...]
    if has_y:
        y_ref = refs.pop(0)
        gpost_ref = refs.pop(0)
        x = x + alpha * _rms(y_ref[...], gpost_ref[...])
    if want_n:
        gpre_ref = refs.pop(0)
    if want_x and two_out:
        op_ref = refs.pop(0)
        os_ref = refs.pop(0)

        @pl.when(i < n_ptiles)
        def _():
            op_ref[...] = x

        @pl.when(i >= n_ptiles)
        def _():
            os_ref[...] = x
    elif want_x:
        refs.pop(0)[...] = x
    if want_n:
        refs.pop(0)[...] = _rms(x, gpre_ref[...]).astype(BF16)


def _resnorm(x, y=None, g_post=None, g_pre=None, *, alpha=1.0, want_x=True, two_out=False, n_prompt=0, tr=256):
    two_in = isinstance(x, tuple)
    d = x[0].shape[1] if two_in else x.shape[1]
    rows = (x[0].shape[0] + x[1].shape[0]) if two_in else x.shape[0]
    n_ptiles = n_prompt // tr
    if two_in or two_out:
        assert n_prompt % tr == 0 and rows - n_prompt == tr
    has_y = y is not None
    want_n = g_pre is not None
    row_spec = pl.BlockSpec((tr, d), lambda i: (i, 0))
    vec_spec = pl.BlockSpec((1, d), lambda i: (0, 0))
    prompt_spec = pl.BlockSpec((tr, d), lambda i: (jnp.minimum(i, n_ptiles - 1), 0))
    sample_spec = pl.BlockSpec((tr, d), lambda i: (0, 0))
    if two_in:
        args, in_specs = [x[0], x[1]], [prompt_spec, sample_spec]
    else:
        args, in_specs = [x], [row_spec]
    if has_y:
        args += [y, g_post.reshape(1, d)]
        in_specs += [row_spec, vec_spec]
    if want_n:
        args.append(g_pre.reshape(1, d))
        in_specs.append(vec_spec)
    out_shape, out_specs = [], []
    if want_x and two_out:
        out_shape += [jax.ShapeDtypeStruct((n_prompt, d), F32), jax.ShapeDtypeStruct((tr, d), F32)]
        out_specs += [prompt_spec, sample_spec]
    elif want_x:
        out_shape.append(jax.ShapeDtypeStruct((rows, d), F32))
        out_specs.append(row_spec)
    if want_n:
        out_shape.append(jax.ShapeDtypeStruct((rows, d), BF16))
        out_specs.append(row_spec)
    return pl.pallas_call(
        functools.partial(_resnorm_kernel, alpha=alpha, has_y=has_y, want_x=want_x, want_n=want_n,
                          two_in=two_in, two_out=two_out, n_ptiles=n_ptiles),
        grid=(rows // tr,), in_specs=in_specs, out_specs=out_specs, out_shape=out_shape,
        compiler_params=_cparams(1), name="resnorm")(*args)


def _ffn_kernel(xn_ref, wg_ref, wu_ref, wd_ref, o_ref):
    j = pl.program_id(1)
    xn = xn_ref[...]
    g = jnp.dot(xn, wg_ref[...], preferred_element_type=F32)
    u = jnp.dot(xn, wu_ref[...], preferred_element_type=F32)
    h = (g * jax.nn.sigmoid(g) * u).astype(BF16)

    @pl.when(j == 0)
    def _():
        o_ref[...] = jnp.zeros_like(o_ref)

    o_ref[...] += jnp.dot(h, wd_ref[...], preferred_element_type=F32)


def _ffn(xn, wg, wu, wd, *, tm=768, tf=256):
    rows, d = xn.shape
    f = wg.shape[1]
    return pl.pallas_call(
        _ffn_kernel,
        grid=(rows // tm, f // tf),
        in_specs=[pl.BlockSpec((tm, d), lambda i, j: (i, 0), pipeline_mode=pl.Buffered(1)),
                  pl.BlockSpec((d, tf), lambda i, j: (0, j)),
                  pl.BlockSpec((d, tf), lambda i, j: (0, j)),
                  pl.BlockSpec((tf, d), lambda i, j: (j, 0))],
        out_specs=pl.BlockSpec((tm, d), lambda i, j: (i, 0)),
        out_shape=jax.ShapeDtypeStruct((rows, d), F32),
        compiler_params=_cparams(2), name="ffn")(xn, wg, wu, wd)


def _mm_kernel(*refs, n_pairs):
    o_ref = refs[-1]
    acc = None
    for p in range(n_pairs):
        d = jnp.dot(refs[2 * p][...], refs[2 * p + 1][...], preferred_element_type=F32)
        acc = d if acc is None else acc + d
    o_ref[...] = acc.astype(o_ref.dtype)


def _mm(pairs, *, tm=768, tn=512, out_dtype=F32):
    rows = pairs[0][0].shape[0]
    n = pairs[0][1].shape[1]
    tn = min(tn, n)
    args, in_specs = [], []
    for a, w in pairs:
        k = a.shape[1]
        args += [a, w]
        in_specs += [pl.BlockSpec((tm, k), lambda i, j: (i, 0)),
                     pl.BlockSpec((k, tn), lambda i, j: (0, j))]
    return pl.pallas_call(
        functools.partial(_mm_kernel, n_pairs=len(pairs)),
        grid=(rows // tm, n // tn),
        in_specs=in_specs,
        out_specs=pl.BlockSpec((tm, tn), lambda i, j: (i, j)),
        out_shape=jax.ShapeDtypeStruct((rows, n), out_dtype),
        compiler_params=_cparams(2), name="mm")(*args)


def _cumsum_rows(x):
    t = x.shape[0]
    row = lax.broadcasted_iota(jnp.int32, x.shape, 0)
    k = 1
    while k < t:
        x = x + jnp.where(row >= k, pltpu.roll(x, k, axis=0), 0.0)
        k *= 2
    return x


def _mlstm_kernel(um_ref, v_ref, o_ref, gt_ref, cbuf_ref, c0_ref, n0_ref, m0_ref,
                  convw_ref, convb_ref, wq_ref, wk_ref, gbias_ref, hn_ref,
                  hm_ref, cout_ref, nout_ref, mout_ref,
                  xp_s, c_s, n_s, m_s, *, t, heads, dqk, dv, n_chunks):
    c = pl.program_id(1)

    @pl.when(c == 0)
    def _():
        xp_s[0:8, :] = cbuf_ref[0]
        c_s[...] = c0_ref[0]
        n_s[...] = n0_ref[0]
        m_s[...] = m0_ref[0]

    xp_s[8:8 + t, :] = um_ref[...]
    cv = convb_ref[...] + xp_s[8 - (CONV_W - 1):8 - (CONV_W - 1) + t, :] * convw_ref[0:1, :]
    for j in range(1, CONV_W):
        off = 8 - (CONV_W - 1) + j
        cv = cv + xp_s[off:off + t, :] * convw_ref[j:j + 1, :]
    cv = cv * jax.nn.sigmoid(cv)
    xp_s[0:8, :] = xp_s[t:t + 8, :]

    gt = gt_ref[...] + gbias_ref[...]
    lane = lax.broadcasted_iota(jnp.int32, gt.shape, 1)
    lf = jnp.minimum(gt, 0.0) - jnp.log1p(jnp.exp(-jnp.abs(gt)))
    g_all = jnp.where(lane < heads, gt, lf)
    b_all = _cumsum_rows(g_all)
    mix = jnp.where(lane < heads, g_all, b_all)
    tp = max(t, 128)
    if tp != t:
        mix_p = jnp.concatenate([mix, jnp.zeros((tp - t, mix.shape[1]), F32)], axis=0)
    else:
        mix_p = mix
    mix_t = mix_p.T[:, :t]

    rr = lax.broadcasted_iota(jnp.int32, (t, t), 0)
    cc = lax.broadcasted_iota(jnp.int32, (t, t), 1)
    causal = cc <= rr

    for h in range(heads):
        ig_c = mix[:, h:h + 1]
        b_c = mix[:, heads + h:heads + h + 1]
        ig_r = mix_t[h:h + 1, :]
        b_r = mix_t[heads + h:heads + h + 1, :]
        m0 = m_s[h:h + 1, 0:1]
        log_d = jnp.where(causal, b_c - b_r + ig_r, -jnp.inf)
        inter = b_c + m0
        m_t = jnp.maximum(inter, jnp.max(log_d, axis=1, keepdims=True))
        dmat = jnp.exp(log_d - m_t)
        cvh = cv[:, h * dv:(h + 1) * dv].astype(BF16)
        q = jnp.dot(cvh, wq_ref[h], preferred_element_type=F32)
        k = jnp.dot(cvh, wk_ref[h], preferred_element_type=F32) * (dqk ** -0.5)
        qb = q.astype(BF16)
        kb = k.astype(BF16)
        s = lax.dot_general(qb, kb, (((1,), (1,)), ((), ())), preferred_element_type=F32) * dmat
        w_inter = jnp.exp(inter - m_t)
        c0 = c_s[h]
        n0 = n_s[h:h + 1, :]
        vh = v_ref[:, h * dv:(h + 1) * dv]
        num = (w_inter * jnp.dot(qb, c0.astype(BF16), preferred_element_type=F32)
               + jnp.dot(s.astype(BF16), vh.astype(BF16), preferred_element_type=F32))
        den = w_inter * jnp.sum(q * n0, axis=1, keepdims=True) + jnp.sum(s, axis=1, keepdims=True)
        hh = num / jnp.maximum(jnp.abs(den), jnp.exp(-m_t))
        b_last = b_c[t - 1:t, :]
        log_w = b_last - b_c + ig_c
        m_new = jnp.maximum(b_last + m0, jnp.max(log_w, axis=0, keepdims=True))
        w_col = jnp.exp(log_w - m_new)
        decay = jnp.exp(b_last + m0 - m_new)
        wv = (w_col * vh).astype(BF16)
        c_s[h] = decay * c0 + lax.dot_general(kb, wv, (((0,), (0,)), ((), ())),
                                              preferred_element_type=F32)
        n_s[h:h + 1, :] = decay * n0 + jnp.sum(w_col * k, axis=0, keepdims=True)
        m_s[h:h + 1, :] = jnp.broadcast_to(m_new, (1, m_s.shape[1]))
        hn = _rms(hh, hn_ref[h:h + 1, :])
        og = jax.nn.sigmoid(o_ref[:, h * dv:(h + 1) * dv])
        hm_ref[:, h * dv:(h + 1) * dv] = (hn * og).astype(BF16)

    @pl.when(c == n_chunks - 1)
    def _():
        cout_ref[0] = c_s[...]
        nout_ref[0] = n_s[...]
        mout_ref[0] = m_s[...]


def _mlstm(p, gates, row0, n_streams, n_chunks, t, cbuf, c0, n0, m0,
           convw, convb, wq, wk, gbias, hnorm):
    heads, dv, dqk = wq.shape
    dm = heads * dv
    blk0 = row0 // t
    assert row0 % t == 0

    def rows(b, c):
        return blk0 + b * n_chunks + c

    kern = functools.partial(_mlstm_kernel, t=t, heads=heads, dqk=dqk, dv=dv, n_chunks=n_chunks)
    n_rows = n_streams * n_chunks * t
    outs = pl.pallas_call(
        kern,
        grid=(n_streams, n_chunks),
        in_specs=[pl.BlockSpec((t, dm), lambda b, c: (rows(b, c), 0)),
                  pl.BlockSpec((t, dm), lambda b, c: (rows(b, c), 1)),
                  pl.BlockSpec((t, dm), lambda b, c: (rows(b, c), 2)),
                  pl.BlockSpec((t, 128), lambda b, c: (rows(b, c), 0)),
                  pl.BlockSpec((1, 8, dm), lambda b, c: (b, 0, 0)),
                  pl.BlockSpec((1, heads, dqk, dv), lambda b, c: (b, 0, 0, 0)),
                  pl.BlockSpec((1, 8, dqk), lambda b, c: (b, 0, 0)),
                  pl.BlockSpec((1, 8, 128), lambda b, c: (b, 0, 0)),
                  pl.BlockSpec((CONV_W, dm), lambda b, c: (0, 0)),
                  pl.BlockSpec((1, dm), lambda b, c: (0, 0)),
                  pl.BlockSpec((heads, dv, dqk), lambda b, c: (0, 0, 0)),
                  pl.BlockSpec((heads, dv, dqk), lambda b, c: (0, 0, 0)),
                  pl.BlockSpec((1, 128), lambda b, c: (0, 0)),
                  pl.BlockSpec((heads, dv), lambda b, c: (0, 0))],
        out_specs=[pl.BlockSpec((t, dm), lambda b, c: (b * n_chunks + c, 0)),
                   pl.BlockSpec((1, heads, dqk, dv), lambda b, c: (b, 0, 0, 0)),
                   pl.BlockSpec((1, 8, dqk), lambda b, c: (b, 0, 0)),
                   pl.BlockSpec((1, 8, 128), lambda b, c: (b, 0, 0))],
        out_shape=[jax.ShapeDtypeStruct((n_rows, dm), BF16),
                   jax.ShapeDtypeStruct((n_streams, heads, dqk, dv), F32),
                   jax.ShapeDtypeStruct((n_streams, 8, dqk), F32),
                   jax.ShapeDtypeStruct((n_streams, 8, 128), F32)],
        scratch_shapes=[pltpu.VMEM((t + 8, dm), F32),
                        pltpu.VMEM((heads, dqk, dv), F32),
                        pltpu.VMEM((8, dqk), F32),
                        pltpu.VMEM((8, 128), F32)],
        compiler_params=_cparams(2), name="mlstm")(
            p, p, p, gates, cbuf, c0, n0, m0, convw, convb, wq, wk, gbias, hnorm)
    return outs


def _s5_kernel(u_ref, ax_re_ref, ax_im_ref, ldtx_ref, bt_re_ref, bt_im_ref, ct_re_ref, ct_im_ref,
               arow_re_ref, arow_im_ref, ldtrow_ref, d_ref, h0_re_ref, h0_im_ref,
               y_ref, hp_re_ref, hp_im_ref, hs_re_ref, hs_im_ref,
               h_re_s, h_im_s, bd_re_s, bd_im_s, cd_re_s, cd_im_s, *, n_prompt, n_win):
    win, grp, st = S5_WIN, S5_GROUP, S5_STATE
    nc = u_ref.shape[1]
    nl = h_re_s.shape[1]

    dt = jnp.exp(ldtx_ref[...])
    ar = ax_re_ref[...]
    ai = ax_im_ref[...]
    mag = jnp.exp(ar * dt)
    ab_re = mag * jnp.cos(ai * dt)
    ab_im = mag * jnp.sin(ai * dt)
    den = ar * ar + ai * ai
    inv_re = ar / den
    inv_im = -ai / den
    f_re = ab_re - 1.0
    f_im = ab_im
    z_re = f_re * inv_re - f_im * inv_im
    z_im = f_re * inv_im + f_im * inv_re
    bb_re = z_re * bt_re_ref[...] - z_im * bt_im_ref[...]
    bb_im = z_re * bt_im_ref[...] + z_im * bt_re_ref[...]
    own_b = (lax.broadcasted_iota(jnp.int32, (nc, nl), 0) // grp
             == lax.broadcasted_iota(jnp.int32, (nc, nl), 1) // st)
    reps_b = nl // bb_re.shape[1]
    bd_re_s[...] = jnp.where(own_b, jnp.concatenate([bb_re] * reps_b, axis=1), 0.0).astype(BF16)
    bd_im_s[...] = jnp.where(own_b, jnp.concatenate([bb_im] * reps_b, axis=1), 0.0).astype(BF16)
    own_c = (lax.broadcasted_iota(jnp.int32, (nl, nc), 0) // st
             == lax.broadcasted_iota(jnp.int32, (nl, nc), 1) // grp)
    reps_c = nc // ct_re_ref.shape[1]
    cd_re_s[...] = jnp.where(own_c, jnp.concatenate([ct_re_ref[...]] * reps_c, axis=1), 0.0).astype(BF16)
    cd_im_s[...] = jnp.where(own_c, -jnp.concatenate([ct_im_ref[...]] * reps_c, axis=1), 0.0).astype(BF16)

    dtr = jnp.exp(ldtrow_ref[...])
    magr = jnp.exp(arow_re_ref[...] * dtr)
    a_re = magr * jnp.cos(arow_im_ref[...] * dtr)
    a_im = magr * jnp.sin(arow_im_ref[...] * dtr)
    aw_re, aw_im = a_re, a_im
    for _ in range(win.bit_length() - 1):
        aw_re, aw_im = aw_re * aw_re - aw_im * aw_im, 2.0 * aw_re * aw_im

    def slab(s):
        return u_ref[pl.ds(s, n_win, stride=win), :]

    def advance(xb):
        hr = h_re_s[...]
        hi = h_im_s[...]
        nr = a_re * hr - a_im * hi + jnp.dot(xb, bd_re_s[...], preferred_element_type=F32)
        ni = a_re * hi + a_im * hr + jnp.dot(xb, bd_im_s[...], preferred_element_type=F32)
        h_re_s[...] = nr
        h_im_s[...] = ni
        return nr, ni

    h_re_s[...] = jnp.zeros_like(h_re_s)
    h_im_s[...] = jnp.zeros_like(h_im_s)

    def step1(s, carry):
        advance(slab(s).astype(BF16))
        return carry

    lax.fori_loop(0, win, step1, 0)

    h0r = h0_re_ref[...]
    h0i = h0_im_ref[...]
    hs_re_ref[...] = aw_re * h0r - aw_im * h0i + h_re_s[n_prompt:n_win, :]
    hs_im_ref[...] = aw_re * h0i + aw_im * h0r + h_im_s[n_prompt:n_win, :]
    h_re_s[n_prompt:n_win, :] = h0r
    h_im_s[n_prompt:n_win, :] = h0i

    def carry_step(c, carry):
        hr, hi = carry
        lr = h_re_s[pl.ds(c, 1), :]
        li = h_im_s[pl.ds(c, 1), :]
        h_re_s[pl.ds(c, 1), :] = hr
        h_im_s[pl.ds(c, 1), :] = hi
        return aw_re * hr - aw_im * hi + lr, aw_re * hi + aw_im * hr + li

    zero = jnp.zeros((1, nl), F32)
    hr, hi = lax.fori_loop(0, n_prompt, carry_step, (zero, zero))
    hp_re_ref[...] = jnp.broadcast_to(hr, hp_re_ref.shape)
    hp_im_ref[...] = jnp.broadcast_to(hi, hp_im_ref.shape)

    def step2(t, carry):
        xt = slab(t)
        nr, ni = advance(xt.astype(BF16))
        y = (jnp.dot(nr.astype(BF16), cd_re_s[...], preferred_element_type=F32)
             + jnp.dot(ni.astype(BF16), cd_im_s[...], preferred_element_type=F32)
             + d_ref[...] * xt)
        y_ref[pl.ds(t, n_win, stride=win), :] = y
        return carry

    lax.fori_loop(0, win, step2, 0)


def _s5(proj, col0, a_re, a_im, log_dt, b_re, b_im, c_re, c_im, d, h0_re, h0_im, *, n_prompt_rows):
    groups, st = a_re.shape
    grp, win, nc = S5_GROUP, S5_WIN, S5_TILE
    gt = nc // grp
    nl = gt * st
    tiles = groups // gt
    rows = proj.shape[0]
    n_win = rows // win
    n_prompt = n_prompt_rows // win
    n_s = n_win - n_prompt
    assert col0 % nc == 0 and (groups * grp) % nc == 0 and win & (win - 1) == 0

    dup = lambda x: jnp.concatenate([x, x], axis=-1)
    rep = lambda x: jnp.repeat(x, grp, axis=0)
    ax_re = rep(dup(a_re))
    ax_im = rep(dup(a_im))
    ldtx = jnp.broadcast_to(rep(log_dt[:, None]), (groups * grp, 2 * st))
    bt_re = dup(jnp.swapaxes(b_re, 1, 2)).reshape(groups * grp, 2 * st)
    bt_im = dup(jnp.swapaxes(b_im, 1, 2)).reshape(groups * grp, 2 * st)
    ct_re = jnp.tile(jnp.swapaxes(c_re, 1, 2), (1, 1, 128 // grp)).reshape(groups * st, 128)
    ct_im = jnp.tile(jnp.swapaxes(c_im, 1, 2), (1, 1, 128 // grp)).reshape(groups * st, 128)
    arow_re = a_re.reshape(tiles, 1, nl)
    arow_im = a_im.reshape(tiles, 1, nl)
    ldtrow = jnp.repeat(log_dt, st).reshape(tiles, 1, nl)
    drow = d.reshape(tiles, 1, nc)

    rowblk = lambda w: pl.BlockSpec((nc, w), lambda j: (j, 0))
    vec = lambda w: pl.BlockSpec((None, 1, w), lambda j: (j, 0, 0))
    lanes = groups * st
    kern = functools.partial(_s5_kernel, n_prompt=n_prompt, n_win=n_win)
    return pl.pallas_call(
        kern, grid=(tiles,),
        in_specs=[pl.BlockSpec((rows, nc), lambda j: (0, col0 // nc + j)),
                  rowblk(2 * st), rowblk(2 * st), rowblk(2 * st), rowblk(2 * st), rowblk(2 * st),
                  pl.BlockSpec((nl, 128), lambda j: (j, 0)), pl.BlockSpec((nl, 128), lambda j: (j, 0)),
                  vec(nl), vec(nl), vec(nl), vec(nc),
                  pl.BlockSpec((n_s, nl), lambda j: (0, j)), pl.BlockSpec((n_s, nl), lambda j: (0, j))],
        out_specs=[pl.BlockSpec((rows, nc), lambda j: (0, j)),
                   pl.BlockSpec((8, nl), lambda j: (0, j)), pl.BlockSpec((8, nl), lambda j: (0, j)),
                   pl.BlockSpec((n_s, nl), lambda j: (0, j)), pl.BlockSpec((n_s, nl), lambda j: (0, j))],
        out_shape=[jax.ShapeDtypeStruct((rows, groups * grp), F32),
                   jax.ShapeDtypeStruct((8, lanes), F32), jax.ShapeDtypeStruct((8, lanes), F32),
                   jax.ShapeDtypeStruct((n_s, lanes), F32), jax.ShapeDtypeStruct((n_s, lanes), F32)],
        scratch_shapes=[pltpu.VMEM((n_win, nl), F32), pltpu.VMEM((n_win, nl), F32),
                        pltpu.VMEM((nc, nl), BF16), pltpu.VMEM((nc, nl), BF16),
                        pltpu.VMEM((nl, nc), BF16), pltpu.VMEM((nl, nc), BF16)],
        compiler_params=_cparams(1), name="s5")(
            proj, ax_re, ax_im, ldtx, bt_re, bt_im, ct_re, ct_im, arow_re, arow_im, ldtrow, drow,
            h0_re, h0_im)


def _glu_kernel(y_ref, w_ref, b_ref, o_ref):
    g = jax.nn.gelu(y_ref[...])
    z = jnp.dot(g.astype(BF16), w_ref[...], preferred_element_type=F32) + b_ref[...]
    o_ref[...] = (g * jax.nn.sigmoid(z)).astype(BF16)


def _glu(y, w, b, *, tm=768):
    rows, d = y.shape
    return pl.pallas_call(
        _glu_kernel, grid=(rows // tm,),
        in_specs=[pl.BlockSpec((tm, d), lambda i: (i, 0)),
                  pl.BlockSpec((d, d), lambda i: (0, 0)),
                  pl.BlockSpec((1, d), lambda i: (0, 0))],
        out_specs=pl.BlockSpec((tm, d), lambda i: (i, 0)),
        out_shape=jax.ShapeDtypeStruct((rows, d), BF16),
        compiler_params=_cparams(1), name="glu")(y, w, b.reshape(1, d))


def _pad_rows8(x, at_end=False):
    b, r, n = x.shape
    z = jnp.zeros((b, 8 - r, n), x.dtype)
    return jnp.concatenate([x, z] if not at_end else [z, x], axis=1)


def kernel(x_prompt, x_sample, state_mlstm_C, state_mlstm_n, state_mlstm_m, cache_mlstm_conv,
           state_s5_re, state_s5_im, ff1_norm_pre, ff1_norm_post, ff1_w_gate, ff1_w_up, ff1_w_down,
           mix_norm_pre, w_in, mlstm_conv_w, mlstm_conv_b, mlstm_w_q, mlstm_w_k, mlstm_b_i, mlstm_b_f,
           mlstm_head_norm, s5_a_re, s5_a_im, s5_log_dt, s5_b_re, s5_b_im, s5_c_re, s5_c_im, s5_d,
           s5_w_glu, s5_b_glu, w_out, mix_norm_post, ff2_norm_pre, ff2_norm_post, ff2_w_gate, ff2_w_up,
           ff2_w_down):
    depth = w_in.shape[0]
    bp, lp, d = x_prompt.shape
    bs, ls, _ = x_sample.shape
    assert bp == 1 and ls == S5_WIN and lp % 256 == 0
    heads, dv, dqk = mlstm_w_q.shape[1:]
    dm = heads * dv
    groups, st = s5_a_re.shape[1:]
    n_p = bp * lp
    n_s = bs * ls
    t_prompt = 256

    x = (x_prompt.reshape(n_p, d), x_sample.reshape(n_s, d))
    new_p, new_s = [], []
    for l in range(depth):
        bf = lambda w: w[l].astype(BF16)
        last = l == depth - 1
        xn = _resnorm(x, g_pre=ff1_norm_pre[l], want_x=False, n_prompt=n_p)[0]
        y = _ffn(xn, bf(ff1_w_gate), bf(ff1_w_up), bf(ff1_w_down))
        x, hmix = _resnorm(x, y, ff1_norm_post[l], mix_norm_pre[l], alpha=0.5, n_prompt=n_p)

        w_in_l = w_in[l]
        n_main = 3 * dm + groups * S5_GROUP
        proj = _mm([(hmix, w_in_l[:, :n_main].astype(BF16))])
        w_gate_cols = jnp.pad(w_in_l[:, n_main:], ((0, 0), (0, 128 - 2 * heads))).astype(BF16)
        gates = _mm([(hmix, w_gate_cols)])
        gbias = jnp.pad(jnp.concatenate([mlstm_b_i[l], mlstm_b_f[l]]), (0, 128 - 2 * heads)).reshape(1, 128)

        wq = bf(mlstm_w_q)
        wk = bf(mlstm_w_k)
        convw = mlstm_conv_w[l]
        convb = mlstm_conv_b[l].reshape(1, dm)
        hnorm = mlstm_head_norm[l]
        zc = jnp.zeros((bp, heads, dqk, dv), F32)
        hm_p, c_p, nn_p, m_p = _mlstm(
            proj, gates, 0, bp, lp // t_prompt, t_prompt,
            jnp.zeros((bp, 8, dm), F32), zc, jnp.zeros((bp, 8, dqk), F32), jnp.zeros((bp, 8, 128), F32),
            convw, convb, wq, wk, gbias, hnorm)
        m0_s = jnp.broadcast_to(_pad_rows8(state_mlstm_m[l][:, :, None]), (bs, 8, 128))
        hm_s, c_s, nn_s, m_s = _mlstm(
            proj, gates, n_p, bs, 1, ls,
            _pad_rows8(cache_mlstm_conv[l], at_end=True), state_mlstm_C[l],
            _pad_rows8(state_mlstm_n[l]), m0_s, convw, convb, wq, wk, gbias, hnorm)
        hm = jnp.concatenate([hm_p, hm_s], axis=0)
        keep = CONV_W - 1
        conv_p = proj[n_p - keep:n_p, :dm].reshape(bp, keep, dm)
        conv_s = proj[n_p:].reshape(bs, ls, -1)[:, ls - keep:, :dm]

        ys, hp_re, hp_im, hs_re, hs_im = _s5(
            proj, 3 * dm, s5_a_re[l], s5_a_im[l], s5_log_dt[l], s5_b_re[l], s5_b_im[l],
            s5_c_re[l], s5_c_im[l], s5_d[l],
            state_s5_re[l].reshape(bs, groups * st), state_s5_im[l].reshape(bs, groups * st),
            n_prompt_rows=n_p)
        ys = _glu(ys, bf(s5_w_glu), s5_b_glu[l])

        w_out_l = bf(w_out)
        mix = _mm([(hm, w_out_l[:dm]), (ys, w_out_l[dm:])])
        x, xn = _resnorm(x, mix, mix_norm_post[l], ff2_norm_pre[l], alpha=1.0)
        y = _ffn(xn, bf(ff2_w_gate), bf(ff2_w_up), bf(ff2_w_down))
        x = _resnorm(x, y, ff2_norm_post[l], alpha=0.5, two_out=last, n_prompt=n_p)
        x = tuple(x) if last else (x[0][:n_p], x[0][n_p:])

        new_p.append((c_p, nn_p[:, :heads], m_p[:, :heads, 0], conv_p,
                      hp_re[:1].reshape(bp, groups, st), hp_im[:1].reshape(bp, groups, st)))
        new_s.append((c_s, nn_s[:, :heads], m_s[:, :heads, 0], conv_s,
                      hs_re.reshape(bs, groups, st), hs_im.reshape(bs, groups, st)))

    outs_p = [jnp.stack([e[i] for e in new_p]) for i in range(6)]
    outs_s = [jnp.stack([e[i] for e in new_s]) for i in range(6)]
    y_prompt = x[0].reshape(bp, lp, d)
    y_sample = x[1].reshape(bs, ls, d)
    return (y_prompt, y_sample, *outs_p, *outs_s)
```

```python
import functools

import jax
import jax.numpy as jnp
from jax import lax
from jax.experimental import pallas as pl
from jax.experimental.pallas import tpu as pltpu

F32 = jnp.float32
BF16 = jnp.bfloat16
RMS_EPS = 1e-6
CONV_W = 4
S5_WIN = 16
S5_GROUP = 16
S5_STATE = 64
S5_TILE = 128
VMEM_LIMIT = 56 * 1024 * 1024


def _cparams(n_axes):
    return pltpu.CompilerParams(dimension_semantics=("arbitrary",) * n_axes,
                                vmem_limit_bytes=VMEM_LIMIT)


def _rms(v, g):
    return v * lax.rsqrt(jnp.mean(v * v, axis=-1, keepdims=True) + RMS_EPS) * g


def _resnorm_kernel(*refs, alpha, has_y, want_x, want_n, two_in, two_out, n_ptiles):
    refs = list(refs)
    i = pl.program_id(0)
    if two_in:
        xp_ref = refs.pop(0)
        xs_ref = refs.pop(0)
        x = jnp.where(i < n_ptiles, xp_ref[...], xs_ref[...])
    else:
        x = refs.pop(0)[...]
    if has_y:
        y_ref = refs.pop(0)
        gpost_ref = refs.pop(0)
        x = x + alpha * _rms(y_ref[...], gpost_ref[...])
    if want_n:
        gpre_ref = refs.pop(0)
    if want_x and two_out:
        op_ref = refs.pop(0)
        os_ref = refs.pop(0)

        @pl.when(i < n_ptiles)
        def _():
            op_ref[...] = x

        @pl.when(i >= n_ptiles)
        def _():
            os_ref[...] = x
    elif want_x:
        refs.pop(0)[...] = x
    if want_n:
        refs.pop(0)[...] = _rms(x, gpre_ref[...]).astype(BF16)


def _resnorm(x, y=None, g_post=None, g_pre=None, *, alpha=1.0, want_x=True, two_out=False, n_prompt=0, tr=256):
    two_in = isinstance(x, tuple)
    d = x[0].shape[1] if two_in else x.shape[1]
    rows = (x[0].shape[0] + x[1].shape[0]) if two_in else x.shape[0]
    n_ptiles = n_prompt // tr
    if two_in or two_out:
        assert n_prompt % tr == 0 and rows - n_prompt == tr
    has_y = y is not None
    want_n = g_pre is not None
    row_spec = pl.BlockSpec((tr, d), lambda i: (i, 0))
    vec_spec = pl.BlockSpec((1, d), lambda i: (0, 0))
    prompt_spec = pl.BlockSpec((tr, d), lambda i: (jnp.minimum(i, n_ptiles - 1), 0))
    sample_spec = pl.BlockSpec((tr, d), lambda i: (0, 0))
    if two_in:
        args, in_specs = [x[0], x[1]], [prompt_spec, sample_spec]
    else:
        args, in_specs = [x], [row_spec]
    if has_y:
        args += [y, g_post.reshape(1, d)]
        in_specs += [row_spec, vec_spec]
    if want_n:
        args.append(g_pre.reshape(1, d))
        in_specs.append(vec_spec)
    out_shape, out_specs = [], []
    if want_x and two_out:
        out_shape += [jax.ShapeDtypeStruct((n_prompt, d), F32), jax.ShapeDtypeStruct((tr, d), F32)]
        out_specs += [prompt_spec, sample_spec]
    elif want_x:
        out_shape.append(jax.ShapeDtypeStruct((rows, d), F32))
        out_specs.append(row_spec)
    if want_n:
        out_shape.append(jax.ShapeDtypeStruct((rows, d), BF16))
        out_specs.append(row_spec)
    return pl.pallas_call(
        functools.partial(_resnorm_kernel, alpha=alpha, has_y=has_y, want_x=want_x, want_n=want_n,
                          two_in=two_in, two_out=two_out, n_ptiles=n_ptiles),
        grid=(rows // tr,), in_specs=in_specs, out_specs=out_specs, out_shape=out_shape,
        compiler_params=_cparams(1), name="resnorm")(*args)


def _ffn_kernel(*refs, n_side, side_steps):
    xn_ref, wg_ref, wu_ref, wd_ref = refs[:4]
    side_in = refs[4:4 + n_side]
    o_ref = refs[4 + n_side]
    side_out = refs[5 + n_side:]
    j = pl.program_id(1)

    @pl.when(j == 0)
    def _():
        o_ref[...] = jnp.zeros_like(o_ref)

    xn = xn_ref[...]
    g = jnp.dot(xn, wg_ref[...], preferred_element_type=F32)
    u = jnp.dot(xn, wu_ref[...], preferred_element_type=F32)
    h = (g * jax.nn.sigmoid(g) * u).astype(BF16)
    o_ref[...] += jnp.dot(h, wd_ref[...], preferred_element_type=F32)

    if n_side:
        @pl.when(pl.program_id(0) * pl.num_programs(1) + j < side_steps)
        def _():
            for src, dst in zip(side_in, side_out):
                dst[...] = src[...].astype(BF16)


def _ffn(xn, wg, wu, wd, side=(), *, tm=768, tf=256):
    rows, d = xn.shape
    f = wg.shape[1]
    nf = f // tf
    n_steps = (rows // tm) * nf
    side_args, side_in_specs, side_out_specs, side_out_shape = [], [], [], []
    side_steps = max([w.shape[0] // rb for w, rb, _ in side], default=0)
    for w, rb, cols in side:
        nblk = w.shape[0] // rb
        assert w.shape[0] % rb == 0 and nblk <= n_steps and rb % 16 == 0 and cols % 128 == 0
        spec = pl.BlockSpec((rb, cols), lambda i, j, nblk=nblk: (jnp.minimum(i * nf + j, nblk - 1), 0))
        side_args.append(w)
        side_in_specs.append(spec)
        side_out_specs.append(spec)
        side_out_shape.append(jax.ShapeDtypeStruct((w.shape[0], cols), BF16))
    outs = pl.pallas_call(
        functools.partial(_ffn_kernel, n_side=len(side), side_steps=side_steps),
        grid=(rows // tm, nf),
        in_specs=[pl.BlockSpec((tm, d), lambda i, j: (i, 0), pipeline_mode=pl.Buffered(1)),
                  pl.BlockSpec((d, tf), lambda i, j: (0, j)),
                  pl.BlockSpec((d, tf), lambda i, j: (0, j)),
                  pl.BlockSpec((tf, d), lambda i, j: (j, 0))] + side_in_specs,
        out_specs=[pl.BlockSpec((tm, d), lambda i, j: (i, 0), pipeline_mode=pl.Buffered(1))] + side_out_specs,
        out_shape=[jax.ShapeDtypeStruct((rows, d), F32)] + side_out_shape,
        compiler_params=_cparams(2), name="ffn")(xn, wg, wu, wd, *side_args)
    return outs[0], outs[1:]


def _mm_kernel(*refs, n_pairs):
    o_ref = refs[-1]
    acc = None
    for p in range(n_pairs):
        d = jnp.dot(refs[2 * p][...], refs[2 * p + 1][...], preferred_element_type=F32)
        acc = d if acc is None else acc + d
    o_ref[...] = acc.astype(o_ref.dtype)


def _mm(pairs, *, tm=768, tn=512, out_dtype=F32):
    rows = pairs[0][0].shape[0]
    n = pairs[0][1].shape[1]
    tn = min(tn, n)
    args, in_specs = [], []
    for a, w, kb in pairs:
        k = a.shape[1]
        args += [a, w]
        in_specs += [pl.BlockSpec((tm, k), lambda i, j: (i, 0)),
                     pl.BlockSpec((k, tn), lambda i, j, kb=kb: (kb, j))]
    return pl.pallas_call(
        functools.partial(_mm_kernel, n_pairs=len(pairs)),
        grid=(rows // tm, n // tn),
        in_specs=in_specs,
        out_specs=pl.BlockSpec((tm, tn), lambda i, j: (i, j)),
        out_shape=jax.ShapeDtypeStruct((rows, n), out_dtype),
        compiler_params=_cparams(2), name="mm")(*args)


def _cumsum_rows(x):
    t = x.shape[0]
    row = lax.broadcasted_iota(jnp.int32, x.shape, 0)
    k = 1
    while k < t:
        x = x + jnp.where(row >= k, pltpu.roll(x, k, axis=0), 0.0)
        k *= 2
    return x


def _mlstm_kernel(um_ref, v_ref, o_ref, gt_ref, cbuf_ref, c0_ref, n0_ref, m0_ref,
                  convw_ref, convb_ref, wq_ref, wk_ref, gbias_ref, hn_ref,
                  hm_ref, cout_ref, nout_ref, mout_ref,
                  xp_s, c_s, n_s, m_s, *, t, heads, dqk, dv, n_chunks):
    c = pl.program_id(1)

    @pl.when(c == 0)
    def _():
        xp_s[0:8, :] = cbuf_ref[0]
        c_s[...] = c0_ref[0]
        n_s[...] = n0_ref[0]
        m_s[...] = m0_ref[0]

    xp_s[8:8 + t, :] = um_ref[...]
    cv = convb_ref[...] + xp_s[8 - (CONV_W - 1):8 - (CONV_W - 1) + t, :] * convw_ref[0:1, :]
    for j in range(1, CONV_W):
        off = 8 - (CONV_W - 1) + j
        cv = cv + xp_s[off:off + t, :] * convw_ref[j:j + 1, :]
    cv = cv * jax.nn.sigmoid(cv)
    xp_s[0:8, :] = xp_s[t:t + 8, :]

    gt = gt_ref[...] + gbias_ref[...]
    lane = lax.broadcasted_iota(jnp.int32, gt.shape, 1)
    lf = jnp.minimum(gt, 0.0) - jnp.log1p(jnp.exp(-jnp.abs(gt)))
    g_all = jnp.where(lane < heads, gt, lf)
    b_all = _cumsum_rows(g_all)
    mix = jnp.where(lane < heads, g_all, b_all)
    tp = max(t, 128)
    if tp != t:
        mix_p = jnp.concatenate([mix, jnp.zeros((tp - t, mix.shape[1]), F32)], axis=0)
    else:
        mix_p = mix
    mix_t = mix_p.T[:, :t]

    rr = lax.broadcasted_iota(jnp.int32, (t, t), 0)
    cc = lax.broadcasted_iota(jnp.int32, (t, t), 1)
    causal = cc <= rr

    for h in range(heads):
        ig_c = mix[:, h:h + 1]
        b_c = mix[:, heads + h:heads + h + 1]
        ig_r = mix_t[h:h + 1, :]
        b_r = mix_t[heads + h:heads + h + 1, :]
        m0 = m_s[h:h + 1, 0:1]
        log_d = jnp.where(causal, b_c - b_r + ig_r, -jnp.inf)
        inter = b_c + m0
        m_t = jnp.maximum(inter, jnp.max(log_d, axis=1, keepdims=True))
        dmat = jnp.exp(log_d - m_t)
        cvh = cv[:, h * dv:(h + 1) * dv].astype(BF16)
        q = jnp.dot(cvh, wq_ref[h], preferred_element_type=F32)
        k = jnp.dot(cvh, wk_ref[h], preferred_element_type=F32) * (dqk ** -0.5)
        qb = q.astype(BF16)
        kb = k.astype(BF16)
        s = lax.dot_general(qb, kb, (((1,), (1,)), ((), ())), preferred_element_type=F32) * dmat
        w_inter = jnp.exp(inter - m_t)
        c0 = c_s[h]
        n0 = n_s[h:h + 1, :]
        vh = v_ref[:, h * dv:(h + 1) * dv]
        num = (w_inter * jnp.dot(qb, c0.astype(BF16), preferred_element_type=F32)
               + jnp.dot(s.astype(BF16), vh.astype(BF16), preferred_element_type=F32))
        den = w_inter * jnp.sum(q * n0, axis=1, keepdims=True) + jnp.sum(s, axis=1, keepdims=True)
        hh = num / jnp.maximum(jnp.abs(den), jnp.exp(-m_t))
        b_last = b_c[t - 1:t, :]
        log_w = b_last - b_c + ig_c
        m_new = jnp.maximum(b_last + m0, jnp.max(log_w, axis=0, keepdims=True))
        w_col = jnp.exp(log_w - m_new)
        decay = jnp.exp(b_last + m0 - m_new)
        wv = (w_col * vh).astype(BF16)
        c_s[h] = decay * c0 + lax.dot_general(kb, wv, (((0,), (0,)), ((), ())),
                                              preferred_element_type=F32)
        n_s[h:h + 1, :] = decay * n0 + jnp.sum(w_col * k, axis=0, keepdims=True)
        m_s[h:h + 1, :] = jnp.broadcast_to(m_new, (1, m_s.shape[1]))
        hn = _rms(hh, hn_ref[h:h + 1, :])
        og = jax.nn.sigmoid(o_ref[:, h * dv:(h + 1) * dv])
        hm_ref[:, h * dv:(h + 1) * dv] = (hn * og).astype(BF16)

    @pl.when(c == n_chunks - 1)
    def _():
        cout_ref[0] = c_s[...]
        nout_ref[0] = n_s[...]
        mout_ref[0] = m_s[...]


def _mlstm(p, gates, row0, n_streams, n_chunks, t, cbuf, c0, n0, m0,
           convw, convb, wq, wk, gbias, hnorm):
    heads, dv, dqk = wq.shape
    dm = heads * dv
    blk0 = row0 // t
    assert row0 % t == 0

    def rows(b, c):
        return blk0 + b * n_chunks + c

    kern = functools.partial(_mlstm_kernel, t=t, heads=heads, dqk=dqk, dv=dv, n_chunks=n_chunks)
    n_rows = n_streams * n_chunks * t
    outs = pl.pallas_call(
        kern,
        grid=(n_streams, n_chunks),
        in_specs=[pl.BlockSpec((t, dm), lambda b, c: (rows(b, c), 0)),
                  pl.BlockSpec((t, dm), lambda b, c: (rows(b, c), 1)),
                  pl.BlockSpec((t, dm), lambda b, c: (rows(b, c), 2)),
                  pl.BlockSpec((t, 128), lambda b, c: (rows(b, c), 0)),
                  pl.BlockSpec((1, 8, dm), lambda b, c: (b, 0, 0)),
                  pl.BlockSpec((1, heads, dqk, dv), lambda b, c: (b, 0, 0, 0)),
                  pl.BlockSpec((1, 8, dqk), lambda b, c: (b, 0, 0)),
                  pl.BlockSpec((1, 8, 128), lambda b, c: (b, 0, 0)),
                  pl.BlockSpec((CONV_W, dm), lambda b, c: (0, 0)),
                  pl.BlockSpec((1, dm), lambda b, c: (0, 0)),
                  pl.BlockSpec((heads, dv, dqk), lambda b, c: (0, 0, 0)),
                  pl.BlockSpec((heads, dv, dqk), lambda b, c: (0, 0, 0)),
                  pl.BlockSpec((1, 128), lambda b, c: (0, 0)),
                  pl.BlockSpec((heads, dv), lambda b, c: (0, 0))],
        out_specs=[pl.BlockSpec((t, dm), lambda b, c: (b * n_chunks + c, 0)),
                   pl.BlockSpec((1, heads, dqk, dv), lambda b, c: (b, 0, 0, 0)),
                   pl.BlockSpec((1, 8, dqk), lambda b, c: (b, 0, 0)),
                   pl.BlockSpec((1, 8, 128), lambda b, c: (b, 0, 0))],
        out_shape=[jax.ShapeDtypeStruct((n_rows, dm), BF16),
                   jax.ShapeDtypeStruct((n_streams, heads, dqk, dv), F32),
                   jax.ShapeDtypeStruct((n_streams, 8, dqk), F32),
                   jax.ShapeDtypeStruct((n_streams, 8, 128), F32)],
        scratch_shapes=[pltpu.VMEM((t + 8, dm), F32),
                        pltpu.VMEM((heads, dqk, dv), F32),
                        pltpu.VMEM((8, dqk), F32),
                        pltpu.VMEM((8, 128), F32)],
        compiler_params=_cparams(2), name="mlstm")(
            p, p, p, gates, cbuf, c0, n0, m0, convw, convb, wq, wk, gbias, hnorm)
    return outs


def _s5_kernel(u_ref, ax_re_ref, ax_im_ref, ldtx_ref, bt_re_ref, bt_im_ref, ct_re_ref, ct_im_ref,
               arow_re_ref, arow_im_ref, ldtrow_ref, d_ref, h0_re_ref, h0_im_ref,
               y_ref, hp_re_ref, hp_im_ref, hs_re_ref, hs_im_ref,
               h_re_s, h_im_s, bd_re_s, bd_im_s, cd_re_s, cd_im_s, *, n_prompt, n_win):
    win, grp, st = S5_WIN, S5_GROUP, S5_STATE
    nc = u_ref.shape[1]
    nl = h_re_s.shape[1]

    dt = jnp.exp(ldtx_ref[...])
    ar = ax_re_ref[...]
    ai = ax_im_ref[...]
    mag = jnp.exp(ar * dt)
    ab_re = mag * jnp.cos(ai * dt)
    ab_im = mag * jnp.sin(ai * dt)
    den = ar * ar + ai * ai
    inv_re = ar / den
    inv_im = -ai / den
    f_re = ab_re - 1.0
    f_im = ab_im
    z_re = f_re * inv_re - f_im * inv_im
    z_im = f_re * inv_im + f_im * inv_re
    bb_re = z_re * bt_re_ref[...] - z_im * bt_im_ref[...]
    bb_im = z_re * bt_im_ref[...] + z_im * bt_re_ref[...]
    own_b = (lax.broadcasted_iota(jnp.int32, (nc, nl), 0) // grp
             == lax.broadcasted_iota(jnp.int32, (nc, nl), 1) // st)
    reps_b = nl // bb_re.shape[1]
    bd_re_s[...] = jnp.where(own_b, jnp.concatenate([bb_re] * reps_b, axis=1), 0.0).astype(BF16)
    bd_im_s[...] = jnp.where(own_b, jnp.concatenate([bb_im] * reps_b, axis=1), 0.0).astype(BF16)
    own_c = (lax.broadcasted_iota(jnp.int32, (nl, nc), 0) // st
             == lax.broadcasted_iota(jnp.int32, (nl, nc), 1) // grp)
    reps_c = nc // ct_re_ref.shape[1]
    cd_re_s[...] = jnp.where(own_c, jnp.concatenate([ct_re_ref[...]] * reps_c, axis=1), 0.0).astype(BF16)
    cd_im_s[...] = jnp.where(own_c, -jnp.concatenate([ct_im_ref[...]] * reps_c, axis=1), 0.0).astype(BF16)

    dtr = jnp.exp(ldtrow_ref[...])
    magr = jnp.exp(arow_re_ref[...] * dtr)
    a_re = magr * jnp.cos(arow_im_ref[...] * dtr)
    a_im = magr * jnp.sin(arow_im_ref[...] * dtr)
    aw_re, aw_im = a_re, a_im
    for _ in range(win.bit_length() - 1):
        aw_re, aw_im = aw_re * aw_re - aw_im * aw_im, 2.0 * aw_re * aw_im

    def slab(s):
        return u_ref[pl.ds(s, n_win, stride=win), :]

    def advance(xb):
        hr = h_re_s[...]
        hi = h_im_s[...]
        nr = a_re * hr - a_im * hi + jnp.dot(xb, bd_re_s[...], preferred_element_type=F32)
        ni = a_re * hi + a_im * hr + jnp.dot(xb, bd_im_s[...], preferred_element_type=F32)
        h_re_s[...] = nr
        h_im_s[...] = ni
        return nr, ni

    h_re_s[...] = jnp.zeros_like(h_re_s)
    h_im_s[...] = jnp.zeros_like(h_im_s)

    def step1(s, carry):
        advance(slab(s).astype(BF16))
        return carry

    lax.fori_loop(0, win, step1, 0)

    h0r = h0_re_ref[...]
    h0i = h0_im_ref[...]
    hs_re_ref[...] = aw_re * h0r - aw_im * h0i + h_re_s[n_prompt:n_win, :]
    hs_im_ref[...] = aw_re * h0i + aw_im * h0r + h_im_s[n_prompt:n_win, :]
    h_re_s[n_prompt:n_win, :] = h0r
    h_im_s[n_prompt:n_win, :] = h0i

    def carry_step(c, carry):
        hr, hi = carry
        lr = h_re_s[pl.ds(c, 1), :]
        li = h_im_s[pl.ds(c, 1), :]
        h_re_s[pl.ds(c, 1), :] = hr
        h_im_s[pl.ds(c, 1), :] = hi
        return aw_re * hr - aw_im * hi + lr, aw_re * hi + aw_im * hr + li

    zero = jnp.zeros((1, nl), F32)
    hr, hi = lax.fori_loop(0, n_prompt, carry_step, (zero, zero))
    hp_re_ref[...] = jnp.broadcast_to(hr, hp_re_ref.shape)
    hp_im_ref[...] = jnp.broadcast_to(hi, hp_im_ref.shape)

    def step2(t, carry):
        xt = slab(t)
        nr, ni = advance(xt.astype(BF16))
        y = (jnp.dot(nr.astype(BF16), cd_re_s[...], preferred_element_type=F32)
             + jnp.dot(ni.astype(BF16), cd_im_s[...], preferred_element_type=F32)
             + d_ref[...] * xt)
        y_ref[pl.ds(t, n_win, stride=win), :] = y
        return carry

    lax.fori_loop(0, win, step2, 0)


def _s5(proj, col0, a_re, a_im, log_dt, b_re, b_im, c_re, c_im, d, h0_re, h0_im, *, n_prompt_rows):
    groups, st = a_re.shape
    grp, win, nc = S5_GROUP, S5_WIN, S5_TILE
    gt = nc // grp
    nl = gt * st
    tiles = groups // gt
    rows = proj.shape[0]
    n_win = rows // win
    n_prompt = n_prompt_rows // win
    n_s = n_win - n_prompt
    assert col0 % nc == 0 and (groups * grp) % nc == 0 and win & (win - 1) == 0

    dup = lambda x: jnp.concatenate([x, x], axis=-1)
    rep = lambda x: jnp.repeat(x, grp, axis=0)
    ax_re = rep(dup(a_re))
    ax_im = rep(dup(a_im))
    ldtx = jnp.broadcast_to(rep(log_dt[:, None]), (groups * grp, 2 * st))
    bt_re = dup(jnp.swapaxes(b_re, 1, 2)).reshape(groups * grp, 2 * st)
    bt_im = dup(jnp.swapaxes(b_im, 1, 2)).reshape(groups * grp, 2 * st)
    ct_re = jnp.tile(jnp.swapaxes(c_re, 1, 2), (1, 1, 128 // grp)).reshape(groups * st, 128)
    ct_im = jnp.tile(jnp.swapaxes(c_im, 1, 2), (1, 1, 128 // grp)).reshape(groups * st, 128)
    arow_re = a_re.reshape(tiles, 1, nl)
    arow_im = a_im.reshape(tiles, 1, nl)
    ldtrow = jnp.repeat(log_dt, st).reshape(tiles, 1, nl)
    drow = d.reshape(tiles, 1, nc)

    rowblk = lambda w: pl.BlockSpec((nc, w), lambda j: (j, 0))
    vec = lambda w: pl.BlockSpec((None, 1, w), lambda j: (j, 0, 0))
    lanes = groups * st
    kern = functools.partial(_s5_kernel, n_prompt=n_prompt, n_win=n_win)
    return pl.pallas_call(
        kern, grid=(tiles,),
        in_specs=[pl.BlockSpec((rows, nc), lambda j: (0, col0 // nc + j)),
                  rowblk(2 * st), rowblk(2 * st), rowblk(2 * st), rowblk(2 * st), rowblk(2 * st),
                  pl.BlockSpec((nl, 128), lambda j: (j, 0)), pl.BlockSpec((nl, 128), lambda j: (j, 0)),
                  vec(nl), vec(nl), vec(nl), vec(nc),
                  pl.BlockSpec((n_s, nl), lambda j: (0, j)), pl.BlockSpec((n_s, nl), lambda j: (0, j))],
        out_specs=[pl.BlockSpec((rows, nc), lambda j: (0, j)),
                   pl.BlockSpec((8, nl), lambda j: (0, j)), pl.BlockSpec((8, nl), lambda j: (0, j)),
                   pl.BlockSpec((n_s, nl), lambda j: (0, j)), pl.BlockSpec((n_s, nl), lambda j: (0, j))],
        out_shape=[jax.ShapeDtypeStruct((rows, groups * grp), F32),
                   jax.ShapeDtypeStruct((8, lanes), F32), jax.ShapeDtypeStruct((8, lanes), F32),
                   jax.ShapeDtypeStruct((n_s, lanes), F32), jax.ShapeDtypeStruct((n_s, lanes), F32)],
        scratch_shapes=[pltpu.VMEM((n_win, nl), F32), pltpu.VMEM((n_win, nl), F32),
                        pltpu.VMEM((nc, nl), BF16), pltpu.VMEM((nc, nl), BF16),
                        pltpu.VMEM((nl, nc), BF16), pltpu.VMEM((nl, nc), BF16)],
        compiler_params=_cparams(1), name="s5")(
            proj, ax_re, ax_im, ldtx, bt_re, bt_im, ct_re, ct_im, arow_re, arow_im, ldtrow, drow,
            h0_re, h0_im)


def _glu_kernel(y_ref, w_ref, b_ref, o_ref):
    g = jax.nn.gelu(y_ref[...])
    z = jnp.dot(g.astype(BF16), w_ref[...], preferred_element_type=F32) + b_ref[...]
    o_ref[...] = (g * jax.nn.sigmoid(z)).astype(BF16)


def _glu(y, w, b, *, tm=768):
    rows, d = y.shape
    return pl.pallas_call(
        _glu_kernel, grid=(rows // tm,),
        in_specs=[pl.BlockSpec((tm, d), lambda i: (i, 0)),
                  pl.BlockSpec((d, d), lambda i: (0, 0)),
                  pl.BlockSpec((1, d), lambda i: (0, 0))],
        out_specs=pl.BlockSpec((tm, d), lambda i: (i, 0)),
        out_shape=jax.ShapeDtypeStruct((rows, d), BF16),
        compiler_params=_cparams(1), name="glu")(y, w, b.reshape(1, d))


def _pad_rows8(x, at_end=False):
    b, r, n = x.shape
    z = jnp.zeros((b, 8 - r, n), x.dtype)
    return jnp.concatenate([x, z] if not at_end else [z, x], axis=1)


def kernel(x_prompt, x_sample, state_mlstm_C, state_mlstm_n, state_mlstm_m, cache_mlstm_conv,
           state_s5_re, state_s5_im, ff1_norm_pre, ff1_norm_post, ff1_w_gate, ff1_w_up, ff1_w_down,
           mix_norm_pre, w_in, mlstm_conv_w, mlstm_conv_b, mlstm_w_q, mlstm_w_k, mlstm_b_i, mlstm_b_f,
           mlstm_head_norm, s5_a_re, s5_a_im, s5_log_dt, s5_b_re, s5_b_im, s5_c_re, s5_c_im, s5_d,
           s5_w_glu, s5_b_glu, w_out, mix_norm_post, ff2_norm_pre, ff2_norm_post, ff2_w_gate, ff2_w_up,
           ff2_w_down):
    depth = w_in.shape[0]
    bp, lp, d = x_prompt.shape
    bs, ls, _ = x_sample.shape
    assert bp == 1 and ls == S5_WIN and lp % 256 == 0
    heads, dv, dqk = mlstm_w_q.shape[1:]
    dm = heads * dv
    groups, st = s5_a_re.shape[1:]
    n_p = bp * lp
    n_s = bs * ls
    t_prompt = 256

    x = (x_prompt.reshape(n_p, d), x_sample.reshape(n_s, d))
    new_p, new_s = [], []
    for l in range(depth):
        bf = lambda w: w[l].astype(BF16)
        last = l == depth - 1
        n_main = 3 * dm + groups * S5_GROUP
        xn = _resnorm(x, g_pre=ff1_norm_pre[l], want_x=False, n_prompt=n_p)[0]
        later = [(ff2_w_gate[l], 16, ff2_w_gate.shape[2]), (ff2_w_up[l], 16, ff2_w_up.shape[2]),
                 (ff2_w_down[l], 64, d), (w_in[l], 16, n_main), (w_out[l], 16, d),
                 (s5_w_glu[l], 16, s5_w_glu.shape[2])]
        y, (wg2, wu2, wd2, w_in_b, w_out_b, w_glu_b) = _ffn(
            xn, bf(ff1_w_gate), bf(ff1_w_up), bf(ff1_w_down), side=later)
        x, hmix = _resnorm(x, y, ff1_norm_post[l], mix_norm_pre[l], alpha=0.5, n_prompt=n_p)

        proj = _mm([(hmix, w_in_b, 0)])
        w_gate_cols = jnp.pad(w_in[l][:, n_main:], ((0, 0), (0, 128 - 2 * heads))).astype(BF16)
        gates = _mm([(hmix, w_gate_cols, 0)])
        gbias = jnp.pad(jnp.concatenate([mlstm_b_i[l], mlstm_b_f[l]]), (0, 128 - 2 * heads)).reshape(1, 128)

        wq = bf(mlstm_w_q)
        wk = bf(mlstm_w_k)
        convw = mlstm_conv_w[l]
        convb = mlstm_conv_b[l].reshape(1, dm)
        hnorm = mlstm_head_norm[l]
        zc = jnp.zeros((bp, heads, dqk, dv), F32)
        hm_p, c_p, nn_p, m_p = _mlstm(
            proj, gates, 0, bp, lp // t_prompt, t_prompt,
            jnp.zeros((bp, 8, dm), F32), zc, jnp.zeros((bp, 8, dqk), F32), jnp.zeros((bp, 8, 128), F32),
            convw, convb, wq, wk, gbias, hnorm)
        m0_s = jnp.broadcast_to(_pad_rows8(state_mlstm_m[l][:, :, None]), (bs, 8, 128))
        hm_s, c_s, nn_s, m_s = _mlstm(
            proj, gates, n_p, bs, 1, ls,
            _pad_rows8(cache_mlstm_conv[l], at_end=True), state_mlstm_C[l],
            _pad_rows8(state_mlstm_n[l]), m0_s, convw, convb, wq, wk, gbias, hnorm)
        hm = jnp.concatenate([hm_p, hm_s], axis=0)
        keep = CONV_W - 1
        conv_p = proj[n_p - keep:n_p, :dm].reshape(bp, keep, dm)
        conv_s = proj[n_p:].reshape(bs, ls, -1)[:, ls - keep:, :dm]

        ys, hp_re, hp_im, hs_re, hs_im = _s5(
            proj, 3 * dm, s5_a_re[l], s5_a_im[l], s5_log_dt[l], s5_b_re[l], s5_b_im[l],
            s5_c_re[l], s5_c_im[l], s5_d[l],
            state_s5_re[l].reshape(bs, groups * st), state_s5_im[l].reshape(bs, groups * st),
            n_prompt_rows=n_p)
        ys = _glu(ys, w_glu_b, s5_b_glu[l])

        assert dm == ys.shape[1]
        mix = _mm([(hm, w_out_b, 0), (ys, w_out_b, 1)])
        x, xn = _resnorm(x, mix, mix_norm_post[l], ff2_norm_pre[l], alpha=1.0)
        y, _ = _ffn(xn, wg2, wu2, wd2)
        x = _resnorm(x, y, ff2_norm_post[l], alpha=0.5, two_out=last, n_prompt=n_p)
        x = tuple(x) if last else (x[0][:n_p], x[0][n_p:])

        new_p.append((c_p, nn_p[:, :heads], m_p[:, :heads, 0], conv_p,
                      hp_re[:1].reshape(bp, groups, st), hp_im[:1].reshape(bp, groups, st)))
        new_s.append((c_s, nn_s[:, :heads], m_s[:, :heads, 0], conv_s,
                      hs_re.reshape(bs, groups, st), hs_im.reshape(bs, groups, st)))

    outs_p = [jnp.stack([e[i] for e in new_p]) for i in range(6)]
    outs_s = [jnp.stack([e[i] for e in new_s]) for i in range(6)]
    y_prompt = x[0].reshape(bp, lp, d)
    y_sample = x[1].reshape(bs, ls, d)
    return (y_prompt, y_sample, *outs_p, *outs_s)
```

```python
import functools

import jax
import jax.numpy as jnp
from jax import lax
from jax.experimental import pallas as pl
from jax.experimental.pallas import tpu as pltpu

F32 = jnp.float32
BF16 = jnp.bfloat16
RMS_EPS = 1e-6
CONV_W = 4
S5_WIN = 16
S5_GROUP = 16
S5_STATE = 64
S5_TILE = 128
VMEM_LIMIT = 56 * 1024 * 1024


def _cparams(n_axes):
    return pltpu.CompilerParams(dimension_semantics=("arbitrary",) * n_axes,
                                vmem_limit_bytes=VMEM_LIMIT)


def _rms(v, g):
    return v * lax.rsqrt(jnp.mean(v * v, axis=-1, keepdims=True) + RMS_EPS) * g


def _resnorm_kernel(*refs, alpha, has_y, want_x, want_n, two_in, two_out, n_ptiles, n_ytiles):
    refs = list(refs)
    i = pl.program_id(0)
    if two_in:
        xp_ref = refs.pop(0)
        xs_ref = refs.pop(0)
        x = jnp.where(i < n_ptiles, xp_ref[...], xs_ref[...])
    else:
        x = refs.pop(0)[...]
    if has_y:
        if n_ytiles:
            ya_ref = refs.pop(0)
            yb_ref = refs.pop(0)
            y = jnp.where(i < n_ytiles, ya_ref[...], yb_ref[...])
        else:
            y = refs.pop(0)[...]
        gpost_ref = refs.pop(0)
        x = x + alpha * _rms(y, gpost_ref[...])
    if want_n:
        gpre_ref = refs.pop(0)
    if want_x and two_out:
        op_ref = refs.pop(0)
        os_ref = refs.pop(0)

        @pl.when(i < n_ptiles)
        def _():
            op_ref[...] = x

        @pl.when(i >= n_ptiles)
        def _():
            os_ref[...] = x
    elif want_x:
        refs.pop(0)[...] = x
    if want_n:
        refs.pop(0)[...] = _rms(x, gpre_ref[...]).astype(BF16)


def _resnorm(x, y=None, g_post=None, g_pre=None, *, alpha=1.0, want_x=True, two_out=False, n_prompt=0, tr=256):
    two_in = isinstance(x, tuple)
    d = x[0].shape[1] if two_in else x.shape[1]
    rows = (x[0].shape[0] + x[1].shape[0]) if two_in else x.shape[0]
    n_ptiles = n_prompt // tr
    if two_in or two_out:
        assert n_prompt % tr == 0 and rows - n_prompt == tr
    has_y = y is not None
    want_n = g_pre is not None
    row_spec = pl.BlockSpec((tr, d), lambda i: (i, 0))
    vec_spec = pl.BlockSpec((1, d), lambda i: (0, 0))
    prompt_spec = pl.BlockSpec((tr, d), lambda i: (jnp.minimum(i, n_ptiles - 1), 0))
    sample_spec = pl.BlockSpec((tr, d), lambda i: (0, 0))
    if two_in:
        args, in_specs = [x[0], x[1]], [prompt_spec, sample_spec]
    else:
        args, in_specs = [x], [row_spec]
    n_ytiles = 0
    if has_y and isinstance(y, tuple):
        n_ytiles = y[0].shape[0] // tr
        assert y[0].shape[0] % tr == 0 and y[0].shape[0] + y[1].shape[0] == rows
        args += [y[0], y[1], g_post.reshape(1, d)]
        in_specs += [pl.BlockSpec((tr, d), lambda i: (jnp.minimum(i, n_ytiles - 1), 0)),
                     pl.BlockSpec((tr, d), lambda i: (jnp.maximum(i - n_ytiles, 0), 0)), vec_spec]
    elif has_y:
        args += [y, g_post.reshape(1, d)]
        in_specs += [row_spec, vec_spec]
    if want_n:
        args.append(g_pre.reshape(1, d))
        in_specs.append(vec_spec)
    out_shape, out_specs = [], []
    if want_x and two_out:
        out_shape += [jax.ShapeDtypeStruct((n_prompt, d), F32), jax.ShapeDtypeStruct((tr, d), F32)]
        out_specs += [prompt_spec, sample_spec]
    elif want_x:
        out_shape.append(jax.ShapeDtypeStruct((rows, d), F32))
        out_specs.append(row_spec)
    if want_n:
        out_shape.append(jax.ShapeDtypeStruct((rows, d), BF16))
        out_specs.append(row_spec)
    return pl.pallas_call(
        functools.partial(_resnorm_kernel, alpha=alpha, has_y=has_y, want_x=want_x, want_n=want_n,
                          two_in=two_in, two_out=two_out, n_ptiles=n_ptiles, n_ytiles=n_ytiles),
        grid=(rows // tr,), in_specs=in_specs, out_specs=out_specs, out_shape=out_shape,
        compiler_params=_cparams(1), name="resnorm")(*args)


def _ffn_kernel(*refs, n_side, side_steps, emit_bf16):
    xn_ref, wg_ref, wu_ref, wd_ref = refs[:4]
    side_in = refs[4:4 + n_side]
    o_ref = refs[4 + n_side]
    n_emit = 3 if emit_bf16 else 0
    emit_out = refs[5 + n_side:5 + n_side + n_emit]
    side_out = refs[5 + n_side + n_emit:]
    j = pl.program_id(1)

    @pl.when(j == 0)
    def _():
        o_ref[...] = jnp.zeros_like(o_ref)

    wg = wg_ref[...].astype(BF16)
    wu = wu_ref[...].astype(BF16)
    wd = wd_ref[...].astype(BF16)
    for dst, w in zip(emit_out, (wg, wu, wd)):
        dst[...] = w
    xn = xn_ref[...]
    g = jnp.dot(xn, wg, preferred_element_type=F32)
    u = jnp.dot(xn, wu, preferred_element_type=F32)
    h = (g * jax.nn.sigmoid(g) * u).astype(BF16)
    o_ref[...] += jnp.dot(h, wd, preferred_element_type=F32)

    if n_side:
        @pl.when(pl.program_id(0) * pl.num_programs(1) + j < side_steps)
        def _():
            for src, dst in zip(side_in, side_out):
                dst[...] = src[...].astype(BF16)


def _ffn(xn, wg, wu, wd, side=(), *, tile0=0, n_tiles=None, emit_bf16=False, tm=768, tf=256):
    d = xn.shape[1]
    f = wg.shape[1]
    nf = f // tf
    n_tiles = xn.shape[0] // tm - tile0 if n_tiles is None else n_tiles
    rows = n_tiles * tm
    n_steps = n_tiles * nf
    side_args, side_in_specs, side_out_specs, side_out_shape = [], [], [], []
    side_steps = max([w.shape[0] // rb for w, rb, _ in side], default=0)
    for w, rb, cols in side:
        nblk = w.shape[0] // rb
        assert w.shape[0] % rb == 0 and nblk <= n_steps and rb % 16 == 0 and cols % 128 == 0
        spec = pl.BlockSpec((rb, cols), lambda i, j, nblk=nblk: (jnp.minimum(i * nf + j, nblk - 1), 0))
        side_args.append(w)
        side_in_specs.append(spec)
        side_out_specs.append(spec)
        side_out_shape.append(jax.ShapeDtypeStruct((w.shape[0], cols), BF16))
    w_specs = [pl.BlockSpec((d, tf), lambda i, j: (0, j)),
               pl.BlockSpec((d, tf), lambda i, j: (0, j)),
               pl.BlockSpec((tf, d), lambda i, j: (j, 0))]
    emit_specs = w_specs if emit_bf16 else []
    emit_shape = [jax.ShapeDtypeStruct(w.shape, BF16) for w in (wg, wu, wd)] if emit_bf16 else []
    outs = pl.pallas_call(
        functools.partial(_ffn_kernel, n_side=len(side), side_steps=side_steps, emit_bf16=emit_bf16),
        grid=(n_tiles, nf),
        in_specs=[pl.BlockSpec((tm, d), lambda i, j: (tile0 + i, 0), pipeline_mode=pl.Buffered(1))]
        + w_specs + side_in_specs,
        out_specs=[pl.BlockSpec((tm, d), lambda i, j: (i, 0), pipeline_mode=pl.Buffered(1))]
        + emit_specs + side_out_specs,
        out_shape=[jax.ShapeDtypeStruct((rows, d), F32)] + emit_shape + side_out_shape,
        compiler_params=_cparams(2), name="ffn")(xn, wg, wu, wd, *side_args)
    n_emit = len(emit_shape)
    return outs[0], outs[1:1 + n_emit], outs[1 + n_emit:]


def _mm_kernel(*refs, n_pairs):
    o_ref = refs[-1]
    acc = None
    for p in range(n_pairs):
        d = jnp.dot(refs[2 * p][...], refs[2 * p + 1][...].astype(BF16), preferred_element_type=F32)
        acc = d if acc is None else acc + d
    o_ref[...] = acc.astype(o_ref.dtype)


def _mm(pairs, *, tm=768, tn=512, out_dtype=F32):
    rows = pairs[0][0].shape[0]
    n = pairs[0][1].shape[1]
    tn = min(tn, n)
    args, in_specs = [], []
    for a, w, kb in pairs:
        k = a.shape[1]
        args += [a, w]
        in_specs += [pl.BlockSpec((tm, k), lambda i, j: (i, 0)),
                     pl.BlockSpec((k, tn), lambda i, j, kb=kb: (kb, j))]
    return pl.pallas_call(
        functools.partial(_mm_kernel, n_pairs=len(pairs)),
        grid=(rows // tm, n // tn),
        in_specs=in_specs,
        out_specs=pl.BlockSpec((tm, tn), lambda i, j: (i, j)),
        out_shape=jax.ShapeDtypeStruct((rows, n), out_dtype),
        compiler_params=_cparams(2), name="mm")(*args)


def _cumsum_rows(x):
    t = x.shape[0]
    row = lax.broadcasted_iota(jnp.int32, x.shape, 0)
    k = 1
    while k < t:
        x = x + jnp.where(row >= k, pltpu.roll(x, k, axis=0), 0.0)
        k *= 2
    return x


def _mlstm_kernel(um_ref, v_ref, o_ref, gt_ref, cbuf_ref, c0_ref, n0_ref, m0_ref,
                  convw_ref, convb_ref, wq_ref, wk_ref, gbias_ref, hn_ref,
                  hm_ref, cout_ref, nout_ref, mout_ref,
                  xp_s, c_s, n_s, m_s, *, t, heads, dqk, dv, n_chunks):
    c = pl.program_id(1)

    @pl.when(c == 0)
    def _():
        xp_s[0:8, :] = cbuf_ref[0]
        c_s[...] = c0_ref[0]
        n_s[...] = n0_ref[0]
        m_s[...] = m0_ref[0]

    xp_s[8:8 + t, :] = um_ref[...]
    cv = convb_ref[...] + xp_s[8 - (CONV_W - 1):8 - (CONV_W - 1) + t, :] * convw_ref[0:1, :]
    for j in range(1, CONV_W):
        off = 8 - (CONV_W - 1) + j
        cv = cv + xp_s[off:off + t, :] * convw_ref[j:j + 1, :]
    cv = cv * jax.nn.sigmoid(cv)
    xp_s[0:8, :] = xp_s[t:t + 8, :]

    gt = gt_ref[...] + gbias_ref[...]
    lane = lax.broadcasted_iota(jnp.int32, gt.shape, 1)
    lf = jnp.minimum(gt, 0.0) - jnp.log1p(jnp.exp(-jnp.abs(gt)))
    g_all = jnp.where(lane < heads, gt, lf)
    b_all = _cumsum_rows(g_all)
    mix = jnp.where(lane < heads, g_all, b_all)
    tp = max(t, 128)
    if tp != t:
        mix_p = jnp.concatenate([mix, jnp.zeros((tp - t, mix.shape[1]), F32)], axis=0)
    else:
        mix_p = mix
    mix_t = mix_p.T[:, :t]

    rr = lax.broadcasted_iota(jnp.int32, (t, t), 0)
    cc = lax.broadcasted_iota(jnp.int32, (t, t), 1)
    causal = cc <= rr

    for h in range(heads):
        ig_c = mix[:, h:h + 1]
        b_c = mix[:, heads + h:heads + h + 1]
        ig_r = mix_t[h:h + 1, :]
        b_r = mix_t[heads + h:heads + h + 1, :]
        m0 = m_s[h:h + 1, 0:1]
        log_d = jnp.where(causal, b_c - b_r + ig_r, -jnp.inf)
        inter = b_c + m0
        m_t = jnp.maximum(inter, jnp.max(log_d, axis=1, keepdims=True))
        dmat = jnp.exp(log_d - m_t)
        cvh = cv[:, h * dv:(h + 1) * dv].astype(BF16)
        q = jnp.dot(cvh, wq_ref[h], preferred_element_type=F32)
        k = jnp.dot(cvh, wk_ref[h], preferred_element_type=F32) * (dqk ** -0.5)
        qb = q.astype(BF16)
        kb = k.astype(BF16)
        s = lax.dot_general(qb, kb, (((1,), (1,)), ((), ())), preferred_element_type=F32) * dmat
        w_inter = jnp.exp(inter - m_t)
        c0 = c_s[h]
        n0 = n_s[h:h + 1, :]
        vh = v_ref[:, h * dv:(h + 1) * dv]
        num = (w_inter * jnp.dot(qb, c0.astype(BF16), preferred_element_type=F32)
               + jnp.dot(s.astype(BF16), vh.astype(BF16), preferred_element_type=F32))
        den = w_inter * jnp.sum(q * n0, axis=1, keepdims=True) + jnp.sum(s, axis=1, keepdims=True)
        hh = num / jnp.maximum(jnp.abs(den), jnp.exp(-m_t))
        b_last = b_c[t - 1:t, :]
        log_w = b_last - b_c + ig_c
        m_new = jnp.maximum(b_last + m0, jnp.max(log_w, axis=0, keepdims=True))
        w_col = jnp.exp(log_w - m_new)
        decay = jnp.exp(b_last + m0 - m_new)
        wv = (w_col * vh).astype(BF16)
        c_s[h] = decay * c0 + lax.dot_general(kb, wv, (((0,), (0,)), ((), ())),
                                              preferred_element_type=F32)
        n_s[h:h + 1, :] = decay * n0 + jnp.sum(w_col * k, axis=0, keepdims=True)
        m_s[h:h + 1, :] = jnp.broadcast_to(m_new, (1, m_s.shape[1]))
        hn = _rms(hh, hn_ref[h:h + 1, :])
        og = jax.nn.sigmoid(o_ref[:, h * dv:(h + 1) * dv])
        hm_ref[:, h * dv:(h + 1) * dv] = (hn * og).astype(BF16)

    @pl.when(c == n_chunks - 1)
    def _():
        cout_ref[0] = c_s[...]
        nout_ref[0] = n_s[...]
        mout_ref[0] = m_s[...]


def _mlstm(p, gates, row0, n_streams, n_chunks, t, cbuf, c0, n0, m0,
           convw, convb, wq, wk, gbias, hnorm):
    heads, dv, dqk = wq.shape
    dm = heads * dv
    blk0 = row0 // t
    assert row0 % t == 0

    def rows(b, c):
        return blk0 + b * n_chunks + c

    kern = functools.partial(_mlstm_kernel, t=t, heads=heads, dqk=dqk, dv=dv, n_chunks=n_chunks)
    n_rows = n_streams * n_chunks * t
    outs = pl.pallas_call(
        kern,
        grid=(n_streams, n_chunks),
        in_specs=[pl.BlockSpec((t, dm), lambda b, c: (rows(b, c), 0)),
                  pl.BlockSpec((t, dm), lambda b, c: (rows(b, c), 1)),
                  pl.BlockSpec((t, dm), lambda b, c: (rows(b, c), 2)),
                  pl.BlockSpec((t, 128), lambda b, c: (rows(b, c), 0)),
                  pl.BlockSpec((1, 8, dm), lambda b, c: (b, 0, 0)),
                  pl.BlockSpec((1, heads, dqk, dv), lambda b, c: (b, 0, 0, 0)),
                  pl.BlockSpec((1, 8, dqk), lambda b, c: (b, 0, 0)),
                  pl.BlockSpec((1, 8, 128), lambda b, c: (b, 0, 0)),
                  pl.BlockSpec((CONV_W, dm), lambda b, c: (0, 0)),
                  pl.BlockSpec((1, dm), lambda b, c: (0, 0)),
                  pl.BlockSpec((heads, dv, dqk), lambda b, c: (0, 0, 0)),
                  pl.BlockSpec((heads, dv, dqk), lambda b, c: (0, 0, 0)),
                  pl.BlockSpec((1, 128), lambda b, c: (0, 0)),
                  pl.BlockSpec((heads, dv), lambda b, c: (0, 0))],
        out_specs=[pl.BlockSpec((t, dm), lambda b, c: (b * n_chunks + c, 0)),
                   pl.BlockSpec((1, heads, dqk, dv), lambda b, c: (b, 0, 0, 0)),
                   pl.BlockSpec((1, 8, dqk), lambda b, c: (b, 0, 0)),
                   pl.BlockSpec((1, 8, 128), lambda b, c: (b, 0, 0))],
        out_shape=[jax.ShapeDtypeStruct((n_rows, dm), BF16),
                   jax.ShapeDtypeStruct((n_streams, heads, dqk, dv), F32),
                   jax.ShapeDtypeStruct((n_streams, 8, dqk), F32),
                   jax.ShapeDtypeStruct((n_streams, 8, 128), F32)],
        scratch_shapes=[pltpu.VMEM((t + 8, dm), F32),
                        pltpu.VMEM((heads, dqk, dv), F32),
                        pltpu.VMEM((8, dqk), F32),
                        pltpu.VMEM((8, 128), F32)],
        compiler_params=_cparams(2), name="mlstm")(
            p, p, p, gates, cbuf, c0, n0, m0, convw, convb, wq, wk, gbias, hnorm)
    return outs


def _s5_kernel(u_ref, ax_re_ref, ax_im_ref, ldtx_ref, bt_re_ref, bt_im_ref, ct_re_ref, ct_im_ref,
               arow_re_ref, arow_im_ref, ldtrow_ref, d_ref, h0_re_ref, h0_im_ref,
               y_ref, hp_re_ref, hp_im_ref, hs_re_ref, hs_im_ref,
               h_re_s, h_im_s, bm_re_s, bm_im_s, cd_re_s, cd_im_s, *, n_prompt, n_win):
    win, grp, st = S5_WIN, S5_GROUP, S5_STATE
    nc = u_ref.shape[1]
    nl = h_re_s.shape[1]

    dt = jnp.exp(ldtx_ref[...])
    ar = ax_re_ref[...]
    ai = ax_im_ref[...]
    mag = jnp.exp(ar * dt)
    ab_re = mag * jnp.cos(ai * dt)
    ab_im = mag * jnp.sin(ai * dt)
    den = ar * ar + ai * ai
    inv_re = ar / den
    inv_im = -ai / den
    f_re = ab_re - 1.0
    f_im = ab_im
    z_re = f_re * inv_re - f_im * inv_im
    z_im = f_re * inv_im + f_im * inv_re
    bb_re = z_re * bt_re_ref[...] - z_im * bt_im_ref[...]
    bb_im = z_re * bt_im_ref[...] + z_im * bt_re_ref[...]
    own_b = (lax.broadcasted_iota(jnp.int32, (nc, nl), 0) // grp
             == lax.broadcasted_iota(jnp.int32, (nc, nl), 1) // st)
    reps_b = nl // bb_re.shape[1]
    bd_re = jnp.where(own_b, jnp.concatenate([bb_re] * reps_b, axis=1), 0.0)
    bd_im = jnp.where(own_b, jnp.concatenate([bb_im] * reps_b, axis=1), 0.0)
    own_c =(lax.broadcasted_iota(jnp.int32, (nl, nc), 0) // st
             == lax.broadcasted_iota(jnp.int32, (nl, nc), 1) // grp)
    reps_c = nc // ct_re_ref.shape[1]
    cd_re_s[...] = jnp.where(own_c, jnp.concatenate([ct_re_ref[...]] * reps_c, axis=1), 0.0).astype(BF16)
    cd_im_s[...] = jnp.where(own_c, -jnp.concatenate([ct_im_ref[...]] * reps_c, axis=1), 0.0).astype(BF16)

    dtr = jnp.exp(ldtrow_ref[...])
    magr = jnp.exp(arow_re_ref[...] * dtr)
    a_re = magr * jnp.cos(arow_im_ref[...] * dtr)
    a_im = magr * jnp.sin(arow_im_ref[...] * dtr)
    aw_re, aw_im = a_re, a_im
    for _ in range(win.bit_length() - 1):
        aw_re, aw_im = aw_re * aw_re - aw_im * aw_im, 2.0 * aw_re * aw_im

    pw_re = jnp.ones_like(a_re)
    pw_im = jnp.zeros_like(a_im)
    for s in range(win - 1, -1, -1):
        bm_re_s[s * nc:(s + 1) * nc, :] = (bd_re * pw_re - bd_im * pw_im).astype(BF16)
        bm_im_s[s * nc:(s + 1) * nc, :] = (bd_re * pw_im + bd_im * pw_re).astype(BF16)
        pw_re, pw_im = pw_re * a_re - pw_im * a_im, pw_re * a_im + pw_im * a_re

    def slab(s):
        return u_ref[pl.ds(s, n_win, stride=win), :]

    u_all = jnp.concatenate([slab(s).astype(BF16) for s in range(win)], axis=1)
    h_re_s[...] = jnp.dot(u_all, bm_re_s[...], preferred_element_type=F32)
    h_im_s[...] = jnp.dot(u_all, bm_im_s[...], preferred_element_type=F32)

    h0r = h0_re_ref[...]
    h0i = h0_im_ref[...]
    hs_re_ref[...] = aw_re * h0r - aw_im * h0i + h_re_s[n_prompt:n_win, :]
    hs_im_ref[...] = aw_re * h0i + aw_im * h0r + h_im_s[n_prompt:n_win, :]
    h_re_s[n_prompt:n_win, :] = h0r
    h_im_s[n_prompt:n_win, :] = h0i

    def carry_step(c, carry):
        hr, hi = carry
        lr = h_re_s[pl.ds(c, 1), :]
        li = h_im_s[pl.ds(c, 1), :]
        h_re_s[pl.ds(c, 1), :] = hr
        h_im_s[pl.ds(c, 1), :] = hi
        return aw_re * hr - aw_im * hi + lr, aw_re * hi + aw_im * hr + li

    zero = jnp.zeros((1, nl), F32)
    hr, hi = lax.fori_loop(0, n_prompt, carry_step, (zero, zero))
    hp_re_ref[...] = jnp.broadcast_to(hr, hp_re_ref.shape)
    hp_im_ref[...] = jnp.broadcast_to(hi, hp_im_ref.shape)

    def step2(t, carry):
        xt = slab(t)
        xb = xt.astype(BF16)
        hr = h_re_s[...]
        hi = h_im_s[...]
        nr = a_re * hr - a_im * hi + jnp.dot(xb, bm_re_s[(win - 1) * nc:, :], preferred_element_type=F32)
        ni = a_re * hi + a_im * hr + jnp.dot(xb, bm_im_s[(win - 1) * nc:, :], preferred_element_type=F32)
        h_re_s[...] = nr
        h_im_s[...] = ni
        y = (jnp.dot(nr.astype(BF16), cd_re_s[...], preferred_element_type=F32)
             + jnp.dot(ni.astype(BF16), cd_im_s[...], preferred_element_type=F32)
             + d_ref[...] * xt)
        y_ref[pl.ds(t, n_win, stride=win), :] = y
        return carry

    lax.fori_loop(0, win, step2, 0)


def _s5(proj, col0, a_re, a_im, log_dt, b_re, b_im, c_re, c_im, d, h0_re, h0_im, *, n_prompt_rows):
    groups, st = a_re.shape
    grp, win, nc = S5_GROUP, S5_WIN, S5_TILE
    gt = nc // grp
    nl = gt * st
    tiles = groups // gt
    rows = proj.shape[0]
    n_win = rows // win
    n_prompt = n_prompt_rows // win
    n_s = n_win - n_prompt
    assert col0 % nc == 0 and (groups * grp) % nc == 0 and win & (win - 1) == 0

    dup = lambda x: jnp.concatenate([x, x], axis=-1)
    rep = lambda x: jnp.repeat(x, grp, axis=0)
    ax_re = rep(dup(a_re))
    ax_im = rep(dup(a_im))
    ldtx = jnp.broadcast_to(rep(log_dt[:, None]), (groups * grp, 2 * st))
    bt_re = dup(jnp.swapaxes(b_re, 1, 2)).reshape(groups * grp, 2 * st)
    bt_im = dup(jnp.swapaxes(b_im, 1, 2)).reshape(groups * grp, 2 * st)
    ct_re = jnp.tile(jnp.swapaxes(c_re, 1, 2), (1, 1, 128 // grp)).reshape(groups * st, 128)
    ct_im = jnp.tile(jnp.swapaxes(c_im, 1, 2), (1, 1, 128 // grp)).reshape(groups * st, 128)
    arow_re = a_re.reshape(tiles, 1, nl)
    arow_im = a_im.reshape(tiles, 1, nl)
    ldtrow = jnp.repeat(log_dt, st).reshape(tiles, 1, nl)
    drow = d.reshape(tiles, 1, nc)

    rowblk = lambda w: pl.BlockSpec((nc, w), lambda j: (j, 0))
    vec = lambda w: pl.BlockSpec((None, 1, w), lambda j: (j, 0, 0))
    lanes = groups * st
    kern = functools.partial(_s5_kernel, n_prompt=n_prompt, n_win=n_win)
    return pl.pallas_call(
        kern, grid=(tiles,),
        in_specs=[pl.BlockSpec((rows, nc), lambda j: (0, col0 // nc + j)),
                  rowblk(2 * st), rowblk(2 * st), rowblk(2 * st), rowblk(2 * st), rowblk(2 * st),
                  pl.BlockSpec((nl, 128), lambda j: (j, 0)), pl.BlockSpec((nl, 128), lambda j: (j, 0)),
                  vec(nl), vec(nl), vec(nl), vec(nc),
                  pl.BlockSpec((n_s, nl), lambda j: (0, j)), pl.BlockSpec((n_s, nl), lambda j: (0, j))],
        out_specs=[pl.BlockSpec((rows, nc), lambda j: (0, j)),
                   pl.BlockSpec((8, nl), lambda j: (0, j)), pl.BlockSpec((8, nl), lambda j: (0, j)),
                   pl.BlockSpec((n_s, nl), lambda j: (0, j)), pl.BlockSpec((n_s, nl), lambda j: (0, j))],
        out_shape=[jax.ShapeDtypeStruct((rows, groups * grp), F32),
                   jax.ShapeDtypeStruct((8, lanes), F32), jax.ShapeDtypeStruct((8, lanes), F32),
                   jax.ShapeDtypeStruct((n_s, lanes), F32), jax.ShapeDtypeStruct((n_s, lanes), F32)],
        scratch_shapes=[pltpu.VMEM((n_win, nl), F32), pltpu.VMEM((n_win, nl), F32),
                        pltpu.VMEM((win * nc, nl), BF16), pltpu.VMEM((win * nc, nl), BF16),
                        pltpu.VMEM((nl, nc), BF16), pltpu.VMEM((nl, nc), BF16)],
        compiler_params=_cparams(1), name="s5")(
            proj, ax_re, ax_im, ldtx, bt_re, bt_im, ct_re, ct_im, arow_re, arow_im, ldtrow, drow,
            h0_re, h0_im)


def _glu_kernel(y_ref, w_ref, b_ref, o_ref):
    g = jax.nn.gelu(y_ref[...])
    z = jnp.dot(g.astype(BF16), w_ref[...], preferred_element_type=F32) + b_ref[...]
    o_ref[...] = (g * jax.nn.sigmoid(z)).astype(BF16)


def _glu(y, w, b, *, tm=768):
    rows, d = y.shape
    return pl.pallas_call(
        _glu_kernel, grid=(rows // tm,),
        in_specs=[pl.BlockSpec((tm, d), lambda i: (i, 0)),
                  pl.BlockSpec((d, d), lambda i: (0, 0)),
                  pl.BlockSpec((1, d), lambda i: (0, 0))],
        out_specs=pl.BlockSpec((tm, d), lambda i: (i, 0)),
        out_shape=jax.ShapeDtypeStruct((rows, d), BF16),
        compiler_params=_cparams(1), name="glu")(y, w, b.reshape(1, d))


def _pad_rows8(x, at_end=False):
    b, r, n = x.shape
    z = jnp.zeros((b, 8 - r, n), x.dtype)
    return jnp.concatenate([x, z] if not at_end else [z, x], axis=1)


def kernel(x_prompt, x_sample, state_mlstm_C, state_mlstm_n, state_mlstm_m, cache_mlstm_conv,
           state_s5_re, state_s5_im, ff1_norm_pre, ff1_norm_post, ff1_w_gate, ff1_w_up, ff1_w_down,
           mix_norm_pre, w_in, mlstm_conv_w, mlstm_conv_b, mlstm_w_q, mlstm_w_k, mlstm_b_i, mlstm_b_f,
           mlstm_head_norm, s5_a_re, s5_a_im, s5_log_dt, s5_b_re, s5_b_im, s5_c_re, s5_c_im, s5_d,
           s5_w_glu, s5_b_glu, w_out, mix_norm_post, ff2_norm_pre, ff2_norm_post, ff2_w_gate, ff2_w_up,
           ff2_w_down):
    depth = w_in.shape[0]
    bp, lp, d = x_prompt.shape
    bs, ls, _ = x_sample.shape
    assert bp == 1 and ls == S5_WIN and lp % 256 == 0
    heads, dv, dqk = mlstm_w_q.shape[1:]
    dm = heads * dv
    groups, st = s5_a_re.shape[1:]
    n_p = bp * lp
    n_s = bs * ls
    t_prompt = 256

    x = (x_prompt.reshape(n_p, d), x_sample.reshape(n_s, d))
    new_p, new_s = [], []
    for l in range(depth):
        bf = lambda w: w[l].astype(BF16)
        last = l == depth - 1
        n_main = 3 * dm + groups * S5_GROUP
        xn = _resnorm(x, g_pre=ff1_norm_pre[l], want_x=False, n_prompt=n_p)[0]
        later = [(ff2_w_gate[l], 16, ff2_w_gate.shape[2]), (ff2_w_up[l], 16, ff2_w_up.shape[2]),
                 (ff2_w_down[l], 64, d), (w_in[l], 16, n_main), (w_out[l], 16, d),
                 (s5_w_glu[l], 16, s5_w_glu.shape[2])]
        y_a, (wg1, wu1, wd1), _ = _ffn(xn, ff1_w_gate[l], ff1_w_up[l], ff1_w_down[l],
                                       n_tiles=1, emit_bf16=True)
        y_b, _, (wg2, wu2, wd2, w_in_b, w_out_b, w_glu_b) = _ffn(xn, wg1, wu1, wd1, side=later, tile0=1)
        x, hmix = _resnorm(x, (y_a, y_b), ff1_norm_post[l], mix_norm_pre[l], alpha=0.5, n_prompt=n_p)

        proj = _mm([(hmix, w_in_b, 0)])
        w_gate_cols = jnp.pad(w_in[l][:, n_main:], ((0, 0), (0, 128 - 2 * heads)))
        gates = _mm([(hmix, w_gate_cols, 0)])
        gbias = jnp.pad(jnp.concatenate([mlstm_b_i[l], mlstm_b_f[l]]), (0, 128 - 2 * heads)).reshape(1, 128)

        wq = bf(mlstm_w_q)
        wk = bf(mlstm_w_k)
        convw = mlstm_conv_w[l]
        convb = mlstm_conv_b[l].reshape(1, dm)
        hnorm = mlstm_head_norm[l]
        zc = jnp.zeros((bp, heads, dqk, dv), F32)
        hm_p, c_p, nn_p, m_p = _mlstm(
            proj, gates, 0, bp, lp // t_prompt, t_prompt,
            jnp.zeros((bp, 8, dm), F32), zc, jnp.zeros((bp, 8, dqk), F32), jnp.zeros((bp, 8, 128), F32),
            convw, convb, wq, wk, gbias, hnorm)
        m0_s = jnp.broadcast_to(_pad_rows8(state_mlstm_m[l][:, :, None]), (bs, 8, 128))
        hm_s, c_s, nn_s, m_s = _mlstm(
            proj, gates, n_p, bs, 1, ls,
            _pad_rows8(cache_mlstm_conv[l], at_end=True), state_mlstm_C[l],
            _pad_rows8(state_mlstm_n[l]), m0_s, convw, convb, wq, wk, gbias, hnorm)
        hm = jnp.concatenate([hm_p, hm_s], axis=0)
        keep = CONV_W - 1
        conv_p = proj[n_p - keep:n_p, :dm].reshape(bp, keep, dm)
        conv_s = proj[n_p:].reshape(bs, ls, -1)[:, ls - keep:, :dm]

        ys, hp_re, hp_im, hs_re, hs_im = _s5(
            proj, 3 * dm, s5_a_re[l], s5_a_im[l], s5_log_dt[l], s5_b_re[l], s5_b_im[l],
            s5_c_re[l], s5_c_im[l], s5_d[l],
            state_s5_re[l].reshape(bs, groups * st), state_s5_im[l].reshape(bs, groups * st),
            n_prompt_rows=n_p)
        ys = _glu(ys, w_glu_b, s5_b_glu[l])

        assert dm == ys.shape[1]
        mix = _mm([(hm, w_out_b, 0), (ys, w_out_b, 1)])
        x, xn = _resnorm(x, mix, mix_norm_post[l], ff2_norm_pre[l], alpha=1.0)
        y = _ffn(xn, wg2, wu2, wd2)[0]
        x = _resnorm(x, y, ff2_norm_post[l], alpha=0.5, two_out=last, n_prompt=n_p)
        x = tuple(x) if last else (x[0][:n_p], x[0][n_p:])

        new_p.append((c_p, nn_p[:, :heads], m_p[:, :heads, 0], conv_p,
                      hp_re[:1].reshape(bp, groups, st), hp_im[:1].reshape(bp, groups, st)))
        new_s.append((c_s, nn_s[:, :heads], m_s[:, :heads, 0], conv_s,
                      hs_re.reshape(bs, groups, st), hs_im.reshape(bs, groups, st)))

    outs_p = [jnp.stack([e[i] for e in new_p]) for i in range(6)]
    outs_s = [jnp.stack([e[i] for e in new_s]) for i in range(6)]
    y_prompt = x[0].reshape(bp, lp, d)
    y_sample = x[1].reshape(bs, ls, d)
    return (y_prompt, y_sample, *outs_p, *outs_s)
```

```python
import functools

import jax
import jax.numpy as jnp
from jax import lax
from jax.experimental import pallas as pl
from jax.experimental.pallas import tpu as pltpu

F32 = jnp.float32
BF16 = jnp.bfloat16
RMS_EPS = 1e-6
CONV_W = 4
S5_WIN = 16
S5_GROUP = 16
S5_STATE = 64
S5_TILE = 128
VMEM_LIMIT = 56 * 1024 * 1024


def _cparams(n_axes):
    return pltpu.CompilerParams(dimension_semantics=("arbitrary",) * n_axes,
                                vmem_limit_bytes=VMEM_LIMIT)


def _rms(v, g):
    return v * lax.rsqrt(jnp.mean(v * v, axis=-1, keepdims=True) + RMS_EPS) * g


def _resnorm_kernel(*refs, alpha, has_y, want_x, want_n, two_in, two_out, n_ptiles, n_ytiles):
    refs = list(refs)
    i = pl.program_id(0)
    if two_in:
        xp_ref = refs.pop(0)
        xs_ref = refs.pop(0)
        x = jnp.where(i < n_ptiles, xp_ref[...], xs_ref[...])
    else:
        x = refs.pop(0)[...]
    if has_y:
        if n_ytiles:
            ya_ref = refs.pop(0)
            yb_ref = refs.pop(0)
            y = jnp.where(i < n_ytiles, ya_ref[...], yb_ref[...])
        else:
            y = refs.pop(0)[...]
        gpost_ref = refs.pop(0)
        x = x + alpha * _rms(y, gpost_ref[...])
    if want_n:
        gpre_ref = refs.pop(0)
    if want_x and two_out:
        op_ref = refs.pop(0)
        os_ref = refs.pop(0)

        @pl.when(i < n_ptiles)
        def _():
            op_ref[...] = x

        @pl.when(i >= n_ptiles)
        def _():
            os_ref[...] = x
    elif want_x:
        refs.pop(0)[...] = x
    if want_n:
        refs.pop(0)[...] = _rms(x, gpre_ref[...]).astype(BF16)


def _resnorm(x, y=None, g_post=None, g_pre=None, *, alpha=1.0, want_x=True, two_out=False, n_prompt=0, tr=256):
    two_in = isinstance(x, tuple)
    d = x[0].shape[1] if two_in else x.shape[1]
    rows = (x[0].shape[0] + x[1].shape[0]) if two_in else x.shape[0]
    n_ptiles = n_prompt // tr
    if two_in or two_out:
        assert n_prompt % tr == 0 and rows - n_prompt == tr
    has_y = y is not None
    want_n = g_pre is not None
    row_spec = pl.BlockSpec((tr, d), lambda i: (i, 0))
    vec_spec = pl.BlockSpec((1, d), lambda i: (0, 0))
    prompt_spec = pl.BlockSpec((tr, d), lambda i: (jnp.minimum(i, n_ptiles - 1), 0))
    sample_spec = pl.BlockSpec((tr, d), lambda i: (0, 0))
    if two_in:
        args, in_specs = [x[0], x[1]], [prompt_spec, sample_spec]
    else:
        args, in_specs = [x], [row_spec]
    n_ytiles = 0
    if has_y and isinstance(y, tuple):
        n_ytiles = y[0].shape[0] // tr
        assert y[0].shape[0] % tr == 0 and y[0].shape[0] + y[1].shape[0] == rows
        args += [y[0], y[1], g_post.reshape(1, d)]
        in_specs += [pl.BlockSpec((tr, d), lambda i: (jnp.minimum(i, n_ytiles - 1), 0)),
                     pl.BlockSpec((tr, d), lambda i: (jnp.maximum(i - n_ytiles, 0), 0)), vec_spec]
    elif has_y:
        args += [y, g_post.reshape(1, d)]
        in_specs += [row_spec, vec_spec]
    if want_n:
        args.append(g_pre.reshape(1, d))
        in_specs.append(vec_spec)
    out_shape, out_specs = [], []
    if want_x and two_out:
        out_shape += [jax.ShapeDtypeStruct((n_prompt, d), F32), jax.ShapeDtypeStruct((tr, d), F32)]
        out_specs += [prompt_spec, sample_spec]
    elif want_x:
        out_shape.append(jax.ShapeDtypeStruct((rows, d), F32))
        out_specs.append(row_spec)
    if want_n:
        out_shape.append(jax.ShapeDtypeStruct((rows, d), BF16))
        out_specs.append(row_spec)
    return pl.pallas_call(
        functools.partial(_resnorm_kernel, alpha=alpha, has_y=has_y, want_x=want_x, want_n=want_n,
                          two_in=two_in, two_out=two_out, n_ptiles=n_ptiles, n_ytiles=n_ytiles),
        grid=(rows // tr,), in_specs=in_specs, out_specs=out_specs, out_shape=out_shape,
        compiler_params=_cparams(1), name="resnorm")(*args)


def _ffn_kernel(*refs, n_side, side_steps, emit_bf16):
    xn_ref, wg_ref, wu_ref, wd_ref = refs[:4]
    side_in = refs[4:4 + n_side]
    o_ref = refs[4 + n_side]
    n_emit = 3 if emit_bf16 else 0
    emit_out = refs[5 + n_side:5 + n_side + n_emit]
    side_out = refs[5 + n_side + n_emit:]
    j = pl.program_id(1)

    @pl.when(j == 0)
    def _():
        o_ref[...] = jnp.zeros_like(o_ref)

    wg = wg_ref[...].astype(BF16)
    wu = wu_ref[...].astype(BF16)
    wd = wd_ref[...].astype(BF16)
    for dst, w in zip(emit_out, (wg, wu, wd)):
        dst[...] = w
    xn = xn_ref[...]
    g = jnp.dot(xn, wg, preferred_element_type=F32)
    u = jnp.dot(xn, wu, preferred_element_type=F32)
    h = (g * jax.nn.sigmoid(g) * u).astype(BF16)
    o_ref[...] += jnp.dot(h, wd, preferred_element_type=F32)

    if n_side:
        @pl.when(pl.program_id(0) * pl.num_programs(1) + j < side_steps)
        def _():
            for src, dst in zip(side_in, side_out):
                dst[...] = src[...].astype(BF16)


def _ffn(xn, wg, wu, wd, side=(), *, tile0=0, n_tiles=None, emit_bf16=False, tm=768, tf=256):
    d = xn.shape[1]
    f = wg.shape[1]
    nf = f // tf
    n_tiles = xn.shape[0] // tm - tile0 if n_tiles is None else n_tiles
    rows = n_tiles * tm
    n_steps = n_tiles * nf
    side_args, side_in_specs, side_out_specs, side_out_shape = [], [], [], []
    side_steps = max([nblk for _, _, nblk, _ in side], default=0)
    for w, rb, nblk, cols in side:
        assert rb * nblk <= w.shape[0] and nblk <= n_steps and rb % 16 == 0 and cols % 128 == 0
        spec = pl.BlockSpec((rb, cols), lambda i, j, nblk=nblk: (jnp.minimum(i * nf + j, nblk - 1), 0))
        side_args.append(w)
        side_in_specs.append(spec)
        side_out_specs.append(spec)
        side_out_shape.append(jax.ShapeDtypeStruct((rb * nblk, cols), BF16))
    w_specs = [pl.BlockSpec((d, tf), lambda i, j: (0, j)),
               pl.BlockSpec((d, tf), lambda i, j: (0, j)),
               pl.BlockSpec((tf, d), lambda i, j: (j, 0))]
    emit_specs = w_specs if emit_bf16 else []
    emit_shape = [jax.ShapeDtypeStruct(w.shape, BF16) for w in (wg, wu, wd)] if emit_bf16 else []
    outs = pl.pallas_call(
        functools.partial(_ffn_kernel, n_side=len(side), side_steps=side_steps, emit_bf16=emit_bf16),
        grid=(n_tiles, nf),
        in_specs=[pl.BlockSpec((tm, d), lambda i, j: (tile0 + i, 0), pipeline_mode=pl.Buffered(1))]
        + w_specs + side_in_specs,
        out_specs=[pl.BlockSpec((tm, d), lambda i, j: (i, 0), pipeline_mode=pl.Buffered(1))]
        + emit_specs + side_out_specs,
        out_shape=[jax.ShapeDtypeStruct((rows, d), F32)] + emit_shape + side_out_shape,
        compiler_params=_cparams(2), name="ffn")(xn, wg, wu, wd, *side_args)
    n_emit = len(emit_shape)
    return outs[0], outs[1:1 + n_emit], outs[1 + n_emit:]


def _mm_kernel(*refs, n_pairs, trans_w):
    o_ref = refs[-1]
    acc = None
    w_dim = 1 if trans_w else 0
    for p in range(n_pairs):
        d = lax.dot_general(refs[2 * p][...], refs[2 * p + 1][...].astype(BF16),
                            (((1,), (w_dim,)), ((), ())), preferred_element_type=F32)
        acc = d if acc is None else acc + d
    o_ref[...] = acc.astype(o_ref.dtype)


def _mm(pairs, *, trans_w=False, tm=768, tn=512, out_dtype=F32):
    rows = pairs[0][0].shape[0]
    n = pairs[0][1].shape[0 if trans_w else 1]
    tn = min(tn, n)
    args, in_specs = [], []
    for a, w, kb in pairs:
        k = a.shape[1]
        args += [a, w]
        in_specs.append(pl.BlockSpec((tm, k), lambda i, j: (i, 0)))
        if trans_w:
            in_specs.append(pl.BlockSpec((tn, k), lambda i, j, kb=kb: (j, kb)))
        else:
            in_specs.append(pl.BlockSpec((k, tn), lambda i, j, kb=kb: (kb, j)))
    return pl.pallas_call(
        functools.partial(_mm_kernel, n_pairs=len(pairs), trans_w=trans_w),
        grid=(rows // tm, n // tn),
        in_specs=in_specs,
        out_specs=pl.BlockSpec((tm, tn), lambda i, j: (i, j)),
        out_shape=jax.ShapeDtypeStruct((rows, n), out_dtype),
        compiler_params=_cparams(2), name="mm")(*args)


def _cumsum_rows(x):
    t = x.shape[0]
    row = lax.broadcasted_iota(jnp.int32, x.shape, 0)
    k = 1
    while k < t:
        x = x + jnp.where(row >= k, pltpu.roll(x, k, axis=0), 0.0)
        k *= 2
    return x


def _mlstm_kernel(um_ref, v_ref, o_ref, gt_ref, cbuf_ref, c0_ref, n0_ref, m0_ref,
                  convw_ref, convb_ref, wq_ref, wk_ref, gbias_ref, hn_ref,
                  hm_ref, cout_ref, nout_ref, mout_ref,
                  xp_s, c_s, n_s, m_s, *, t, heads, dqk, dv, n_chunks):
    c = pl.program_id(1)

    @pl.when(c == 0)
    def _():
        xp_s[0:8, :] = cbuf_ref[0]
        c_s[...] = c0_ref[0]
        n_s[...] = n0_ref[0]
        m_s[...] = m0_ref[0]

    xp_s[8:8 + t, :] = um_ref[...]
    cv = convb_ref[...] + xp_s[8 - (CONV_W - 1):8 - (CONV_W - 1) + t, :] * convw_ref[0:1, :]
    for j in range(1, CONV_W):
        off = 8 - (CONV_W - 1) + j
        cv = cv + xp_s[off:off + t, :] * convw_ref[j:j + 1, :]
    cv = cv * jax.nn.sigmoid(cv)
    xp_s[0:8, :] = xp_s[t:t + 8, :]

    gt = gt_ref[...] + gbias_ref[...]
    lane = lax.broadcasted_iota(jnp.int32, gt.shape, 1)
    lf = jnp.minimum(gt, 0.0) - jnp.log1p(jnp.exp(-jnp.abs(gt)))
    g_all = jnp.where(lane < heads, gt, lf)
    b_all = _cumsum_rows(g_all)
    mix = jnp.where(lane < heads, g_all, b_all)
    tp = max(t, 128)
    if tp != t:
        mix_p = jnp.concatenate([mix, jnp.zeros((tp - t, mix.shape[1]), F32)], axis=0)
    else:
        mix_p = mix
    mix_t = mix_p.T[:, :t]

    rr = lax.broadcasted_iota(jnp.int32, (t, t), 0)
    cc = lax.broadcasted_iota(jnp.int32, (t, t), 1)
    causal = cc <= rr

    for h in range(heads):
        ig_c = mix[:, h:h + 1]
        b_c = mix[:, heads + h:heads + h + 1]
        ig_r = mix_t[h:h + 1, :]
        b_r = mix_t[heads + h:heads + h + 1, :]
        m0 = m_s[h:h + 1, 0:1]
        log_d = jnp.where(causal, b_c - b_r + ig_r, -jnp.inf)
        inter = b_c + m0
        m_t = jnp.maximum(inter, jnp.max(log_d, axis=1, keepdims=True))
        dmat = jnp.exp(log_d - m_t)
        cvh = cv[:, h * dv:(h + 1) * dv].astype(BF16)
        q = jnp.dot(cvh, wq_ref[h], preferred_element_type=F32)
        k = jnp.dot(cvh, wk_ref[h], preferred_element_type=F32) * (dqk ** -0.5)
        qb = q.astype(BF16)
        kb = k.astype(BF16)
        s = lax.dot_general(qb, kb, (((1,), (1,)), ((), ())), preferred_element_type=F32) * dmat
        w_inter = jnp.exp(inter - m_t)
        c0 = c_s[h]
        n0 = n_s[h:h + 1, :]
        vh = v_ref[:, h * dv:(h + 1) * dv]
        num = (w_inter * jnp.dot(qb, c0.astype(BF16), preferred_element_type=F32)
               + jnp.dot(s.astype(BF16), vh.astype(BF16), preferred_element_type=F32))
        den = w_inter * jnp.sum(q * n0, axis=1, keepdims=True) + jnp.sum(s, axis=1, keepdims=True)
        hh = num / jnp.maximum(jnp.abs(den), jnp.exp(-m_t))
        b_last = b_c[t - 1:t, :]
        log_w = b_last - b_c + ig_c
        m_new = jnp.maximum(b_last + m0, jnp.max(log_w, axis=0, keepdims=True))
        w_col = jnp.exp(log_w - m_new)
        decay = jnp.exp(b_last + m0 - m_new)
        wv = (w_col * vh).astype(BF16)
        c_s[h] = decay * c0 + lax.dot_general(kb, wv, (((0,), (0,)), ((), ())),
                                              preferred_element_type=F32)
        n_s[h:h + 1, :] = decay * n0 + jnp.sum(w_col * k, axis=0, keepdims=True)
        m_s[h:h + 1, :] = jnp.broadcast_to(m_new, (1, m_s.shape[1]))
        hn = _rms(hh, hn_ref[h:h + 1, :])
        og = jax.nn.sigmoid(o_ref[:, h * dv:(h + 1) * dv])
        hm_ref[:, h * dv:(h + 1) * dv] = (hn * og).astype(BF16)

    @pl.when(c == n_chunks - 1)
    def _():
        cout_ref[0] = c_s[...]
        nout_ref[0] = n_s[...]
        mout_ref[0] = m_s[...]


def _mlstm(p, gates, row0, n_streams, n_chunks, t, cbuf, c0, n0, m0,
           convw, convb, wq, wk, gbias, hnorm):
    heads, dv, dqk = wq.shape
    dm = heads * dv
    blk0 = row0 // t
    assert row0 % t == 0

    def rows(b, c):
        return blk0 + b * n_chunks + c

    kern = functools.partial(_mlstm_kernel, t=t, heads=heads, dqk=dqk, dv=dv, n_chunks=n_chunks)
    n_rows = n_streams * n_chunks * t
    outs = pl.pallas_call(
        kern,
        grid=(n_streams, n_chunks),
        in_specs=[pl.BlockSpec((t, dm), lambda b, c: (rows(b, c), 0)),
                  pl.BlockSpec((t, dm), lambda b, c: (rows(b, c), 1)),
                  pl.BlockSpec((t, dm), lambda b, c: (rows(b, c), 2)),
                  pl.BlockSpec((t, 128), lambda b, c: (rows(b, c), 0)),
                  pl.BlockSpec((1, 8, dm), lambda b, c: (b, 0, 0)),
                  pl.BlockSpec((1, heads, dqk, dv), lambda b, c: (b, 0, 0, 0)),
                  pl.BlockSpec((1, 8, dqk), lambda b, c: (b, 0, 0)),
                  pl.BlockSpec((1, 8, 128), lambda b, c: (b, 0, 0)),
                  pl.BlockSpec((CONV_W, dm), lambda b, c: (0, 0)),
                  pl.BlockSpec((1, dm), lambda b, c: (0, 0)),
                  pl.BlockSpec((heads, dv, dqk), lambda b, c: (0, 0, 0)),
                  pl.BlockSpec((heads, dv, dqk), lambda b, c: (0, 0, 0)),
                  pl.BlockSpec((1, 128), lambda b, c: (0, 0)),
                  pl.BlockSpec((heads, dv), lambda b, c: (0, 0))],
        out_specs=[pl.BlockSpec((t, dm), lambda b, c: (b * n_chunks + c, 0)),
                   pl.BlockSpec((1, heads, dqk, dv), lambda b, c: (b, 0, 0, 0)),
                   pl.BlockSpec((1, 8, dqk), lambda b, c: (b, 0, 0)),
                   pl.BlockSpec((1, 8, 128), lambda b, c: (b, 0, 0))],
        out_shape=[jax.ShapeDtypeStruct((n_rows, dm), BF16),
                   jax.ShapeDtypeStruct((n_streams, heads, dqk, dv), F32),
                   jax.ShapeDtypeStruct((n_streams, 8, dqk), F32),
                   jax.ShapeDtypeStruct((n_streams, 8, 128), F32)],
        scratch_shapes=[pltpu.VMEM((t + 8, dm), F32),
                        pltpu.VMEM((heads, dqk, dv), F32),
                        pltpu.VMEM((8, dqk), F32),
                        pltpu.VMEM((8, 128), F32)],
        compiler_params=_cparams(2), name="mlstm")(
            p, p, p, gates, cbuf, c0, n0, m0, convw, convb, wq, wk, gbias, hnorm)
    return outs


def _s5_kernel(u_ref, ax_re_ref, ax_im_ref, ldtx_ref, bt_re_ref, bt_im_ref, ct_re_ref, ct_im_ref,
               arow_re_ref, arow_im_ref, ldtrow_ref, d_ref, h0_re_ref, h0_im_ref,
               y_ref, hp_re_ref, hp_im_ref, hs_re_ref, hs_im_ref,
               h_re_s, h_im_s, bm_re_s, bm_im_s, cd_re_s, cd_im_s, *, n_prompt, n_win):
    win, grp, st = S5_WIN, S5_GROUP, S5_STATE
    nc = u_ref.shape[1]
    nl = h_re_s.shape[1]

    dt = jnp.exp(ldtx_ref[...])
    ar = ax_re_ref[...]
    ai = ax_im_ref[...]
    mag = jnp.exp(ar * dt)
    ab_re = mag * jnp.cos(ai * dt)
    ab_im = mag * jnp.sin(ai * dt)
    den = ar * ar + ai * ai
    inv_re = ar / den
    inv_im = -ai / den
    f_re = ab_re - 1.0
    f_im = ab_im
    z_re = f_re * inv_re - f_im * inv_im
    z_im = f_re * inv_im + f_im * inv_re
    bb_re = z_re * bt_re_ref[...] - z_im * bt_im_ref[...]
    bb_im = z_re * bt_im_ref[...] + z_im * bt_re_ref[...]
    own_b = (lax.broadcasted_iota(jnp.int32, (nc, nl), 0) // grp
             == lax.broadcasted_iota(jnp.int32, (nc, nl), 1) // st)
    reps_b = nl // bb_re.shape[1]
    bd_re = jnp.where(own_b, jnp.concatenate([bb_re] * reps_b, axis=1), 0.0)
    bd_im = jnp.where(own_b, jnp.concatenate([bb_im] * reps_b, axis=1), 0.0)
    own_c =(lax.broadcasted_iota(jnp.int32, (nl, nc), 0) // st
             == lax.broadcasted_iota(jnp.int32, (nl, nc), 1) // grp)
    reps_c = nc // ct_re_ref.shape[1]
    cd_re_s[...] = jnp.where(own_c, jnp.concatenate([ct_re_ref[...]] * reps_c, axis=1), 0.0).astype(BF16)
    cd_im_s[...] = jnp.where(own_c, -jnp.concatenate([ct_im_ref[...]] * reps_c, axis=1), 0.0).astype(BF16)

    dtr = jnp.exp(ldtrow_ref[...])
    magr = jnp.exp(arow_re_ref[...] * dtr)
    a_re = magr * jnp.cos(arow_im_ref[...] * dtr)
    a_im = magr * jnp.sin(arow_im_ref[...] * dtr)
    aw_re, aw_im = a_re, a_im
    for _ in range(win.bit_length() - 1):
        aw_re, aw_im = aw_re * aw_re - aw_im * aw_im, 2.0 * aw_re * aw_im

    pw_re = jnp.ones_like(a_re)
    pw_im = jnp.zeros_like(a_im)
    for s in range(win - 1, -1, -1):
        bm_re_s[s * nc:(s + 1) * nc, :] = (bd_re * pw_re - bd_im * pw_im).astype(BF16)
        bm_im_s[s * nc:(s + 1) * nc, :] = (bd_re * pw_im + bd_im * pw_re).astype(BF16)
        pw_re, pw_im = pw_re * a_re - pw_im * a_im, pw_re * a_im + pw_im * a_re

    def slab(s):
        return u_ref[pl.ds(s, n_win, stride=win), :]

    u_all = jnp.concatenate([slab(s).astype(BF16) for s in range(win)], axis=1)
    h_re_s[...] = jnp.dot(u_all, bm_re_s[...], preferred_element_type=F32)
    h_im_s[...] = jnp.dot(u_all, bm_im_s[...], preferred_element_type=F32)

    h0r = h0_re_ref[...]
    h0i = h0_im_ref[...]
    hs_re_ref[...] = aw_re * h0r - aw_im * h0i + h_re_s[n_prompt:n_win, :]
    hs_im_ref[...] = aw_re * h0i + aw_im * h0r + h_im_s[n_prompt:n_win, :]
    h_re_s[n_prompt:n_win, :] = h0r
    h_im_s[n_prompt:n_win, :] = h0i

    def carry_step(c, carry):
        hr, hi = carry
        lr = h_re_s[pl.ds(c, 1), :]
        li = h_im_s[pl.ds(c, 1), :]
        h_re_s[pl.ds(c, 1), :] = hr
        h_im_s[pl.ds(c, 1), :] = hi
        return aw_re * hr - aw_im * hi + lr, aw_re * hi + aw_im * hr + li

    zero = jnp.zeros((1, nl), F32)
    hr, hi = lax.fori_loop(0, n_prompt, carry_step, (zero, zero))
    hp_re_ref[...] = jnp.broadcast_to(hr, hp_re_ref.shape)
    hp_im_ref[...] = jnp.broadcast_to(hi, hp_im_ref.shape)

    def step2(t, carry):
        xt = slab(t)
        xb = xt.astype(BF16)
        hr = h_re_s[...]
        hi = h_im_s[...]
        nr = a_re * hr - a_im * hi + jnp.dot(xb, bm_re_s[(win - 1) * nc:, :], preferred_element_type=F32)
        ni = a_re * hi + a_im * hr + jnp.dot(xb, bm_im_s[(win - 1) * nc:, :], preferred_element_type=F32)
        h_re_s[...] = nr
        h_im_s[...] = ni
        y = (jnp.dot(nr.astype(BF16), cd_re_s[...], preferred_element_type=F32)
             + jnp.dot(ni.astype(BF16), cd_im_s[...], preferred_element_type=F32)
             + d_ref[...] * xt)
        y_ref[pl.ds(t, n_win, stride=win), :] = y
        return carry

    lax.fori_loop(0, win, step2, 0, unroll=2)


def _s5(proj, col0, a_re, a_im, log_dt, b_re, b_im, c_re, c_im, d, h0_re, h0_im, *, n_prompt_rows):
    groups, st = a_re.shape
    grp, win, nc = S5_GROUP, S5_WIN, S5_TILE
    gt = nc // grp
    nl = gt * st
    tiles = groups // gt
    rows = proj.shape[0]
    n_win = rows // win
    n_prompt = n_prompt_rows // win
    n_s = n_win - n_prompt
    assert col0 % nc == 0 and (groups * grp) % nc == 0 and win & (win - 1) == 0

    dup = lambda x: jnp.concatenate([x, x], axis=-1)
    rep = lambda x: jnp.repeat(x, grp, axis=0)
    ax_re = rep(dup(a_re))
    ax_im = rep(dup(a_im))
    ldtx = jnp.broadcast_to(rep(log_dt[:, None]), (groups * grp, 2 * st))
    bt_re = dup(jnp.swapaxes(b_re, 1, 2)).reshape(groups * grp, 2 * st)
    bt_im = dup(jnp.swapaxes(b_im, 1, 2)).reshape(groups * grp, 2 * st)
    ct_re = jnp.tile(jnp.swapaxes(c_re, 1, 2), (1, 1, 128 // grp)).reshape(groups * st, 128)
    ct_im = jnp.tile(jnp.swapaxes(c_im, 1, 2), (1, 1, 128 // grp)).reshape(groups * st, 128)
    arow_re = a_re.reshape(tiles, 1, nl)
    arow_im = a_im.reshape(tiles, 1, nl)
    ldtrow = jnp.repeat(log_dt, st).reshape(tiles, 1, nl)
    drow = d.reshape(tiles, 1, nc)

    rowblk = lambda w: pl.BlockSpec((nc, w), lambda j: (j, 0))
    vec = lambda w: pl.BlockSpec((None, 1, w), lambda j: (j, 0, 0))
    lanes = groups * st
    kern = functools.partial(_s5_kernel, n_prompt=n_prompt, n_win=n_win)
    return pl.pallas_call(
        kern, grid=(tiles,),
        in_specs=[pl.BlockSpec((rows, nc), lambda j: (0, col0 // nc + j)),
                  rowblk(2 * st), rowblk(2 * st), rowblk(2 * st), rowblk(2 * st), rowblk(2 * st),
                  pl.BlockSpec((nl, 128), lambda j: (j, 0)), pl.BlockSpec((nl, 128), lambda j: (j, 0)),
                  vec(nl), vec(nl), vec(nl), vec(nc),
                  pl.BlockSpec((n_s, nl), lambda j: (0, j)), pl.BlockSpec((n_s, nl), lambda j: (0, j))],
        out_specs=[pl.BlockSpec((rows, nc), lambda j: (0, j)),
                   pl.BlockSpec((8, nl), lambda j: (0, j)), pl.BlockSpec((8, nl), lambda j: (0, j)),
                   pl.BlockSpec((n_s, nl), lambda j: (0, j)), pl.BlockSpec((n_s, nl), lambda j: (0, j))],
        out_shape=[jax.ShapeDtypeStruct((rows, groups * grp), F32),
                   jax.ShapeDtypeStruct((8, lanes), F32), jax.ShapeDtypeStruct((8, lanes), F32),
                   jax.ShapeDtypeStruct((n_s, lanes), F32), jax.ShapeDtypeStruct((n_s, lanes), F32)],
        scratch_shapes=[pltpu.VMEM((n_win, nl), F32), pltpu.VMEM((n_win, nl), F32),
                        pltpu.VMEM((win * nc, nl), BF16), pltpu.VMEM((win * nc, nl), BF16),
                        pltpu.VMEM((nl, nc), BF16), pltpu.VMEM((nl, nc), BF16)],
        compiler_params=_cparams(1), name="s5")(
            proj, ax_re, ax_im, ldtx, bt_re, bt_im, ct_re, ct_im, arow_re, arow_im, ldtrow, drow,
            h0_re, h0_im)


def _glu_kernel(y_ref, w_ref, b_ref, o_ref):
    g = jax.nn.gelu(y_ref[...])
    z = jnp.dot(g.astype(BF16), w_ref[...], preferred_element_type=F32) + b_ref[...]
    o_ref[...] = (g * jax.nn.sigmoid(z)).astype(BF16)


def _glu(y, w, b, *, tm=768):
    rows, d = y.shape
    return pl.pallas_call(
        _glu_kernel, grid=(rows // tm,),
        in_specs=[pl.BlockSpec((tm, d), lambda i: (i, 0)),
                  pl.BlockSpec((d, d), lambda i: (0, 0)),
                  pl.BlockSpec((1, d), lambda i: (0, 0))],
        out_specs=pl.BlockSpec((tm, d), lambda i: (i, 0)),
        out_shape=jax.ShapeDtypeStruct((rows, d), BF16),
        compiler_params=_cparams(1), name="glu")(y, w, b.reshape(1, d))


def _pad_rows8(x, at_end=False):
    b, r, n = x.shape
    z = jnp.zeros((b, 8 - r, n), x.dtype)
    return jnp.concatenate([x, z] if not at_end else [z, x], axis=1)


def kernel(x_prompt, x_sample, state_mlstm_C, state_mlstm_n, state_mlstm_m, cache_mlstm_conv,
           state_s5_re, state_s5_im, ff1_norm_pre, ff1_norm_post, ff1_w_gate, ff1_w_up, ff1_w_down,
           mix_norm_pre, w_in, mlstm_conv_w, mlstm_conv_b, mlstm_w_q, mlstm_w_k, mlstm_b_i, mlstm_b_f,
           mlstm_head_norm, s5_a_re, s5_a_im, s5_log_dt, s5_b_re, s5_b_im, s5_c_re, s5_c_im, s5_d,
           s5_w_glu, s5_b_glu, w_out, mix_norm_post, ff2_norm_pre, ff2_norm_post, ff2_w_gate, ff2_w_up,
           ff2_w_down):
    depth = w_in.shape[0]
    bp, lp, d = x_prompt.shape
    bs, ls, _ = x_sample.shape
    assert bp == 1 and ls == S5_WIN and lp % 256 == 0
    heads, dv, dqk = mlstm_w_q.shape[1:]
    dm = heads * dv
    groups, st = s5_a_re.shape[1:]
    n_p = bp * lp
    n_s = bs * ls
    t_prompt = 256

    x = (x_prompt.reshape(n_p, d), x_sample.reshape(n_s, d))
    new_p, new_s = [], []
    for l in range(depth):
        bf = lambda w: w[l].astype(BF16)
        last = l == depth - 1
        n_main = 3 * dm + groups * S5_GROUP
        xn = _resnorm(x, g_pre=ff1_norm_pre[l], want_x=False, n_prompt=n_p)[0]
        w_in_t = jnp.swapaxes(w_in[l], 0, 1)
        d_ff = ff2_w_gate.shape[2]
        later = [(ff2_w_gate[l], 16, d // 16, d_ff), (ff2_w_up[l], 16, d // 16, d_ff),
                 (ff2_w_down[l], 64, d_ff // 64, d), (w_in_t, 32, n_main // 32, d),
                 (w_out[l], 16, d // 16, d), (s5_w_glu[l], 16, s5_w_glu.shape[1] // 16, s5_w_glu.shape[2])]
        y_a, (wg1, wu1, wd1), _ = _ffn(xn, ff1_w_gate[l], ff1_w_up[l], ff1_w_down[l],
                                       n_tiles=1, emit_bf16=True)
        y_b, _, (wg2, wu2, wd2, w_in_b, w_out_b, w_glu_b) = _ffn(xn, wg1, wu1, wd1, side=later, tile0=1)
        x, hmix = _resnorm(x, (y_a, y_b), ff1_norm_post[l], mix_norm_pre[l], alpha=0.5, n_prompt=n_p)

        proj = _mm([(hmix, w_in_b, 0)], trans_w=True)
        w_gate_rows = jnp.pad(w_in_t[n_main:], ((0, 128 - 2 * heads), (0, 0)))
        gates = _mm([(hmix, w_gate_rows, 0)], trans_w=True)
        gbias = jnp.pad(jnp.concatenate([mlstm_b_i[l], mlstm_b_f[l]]), (0, 128 - 2 * heads)).reshape(1, 128)

        wq = bf(mlstm_w_q)
        wk = bf(mlstm_w_k)
        convw = mlstm_conv_w[l]
        convb = mlstm_conv_b[l].reshape(1, dm)
        hnorm = mlstm_head_norm[l]
        zc = jnp.zeros((bp, heads, dqk, dv), F32)
        hm_p, c_p, nn_p, m_p = _mlstm(
            proj, gates, 0, bp, lp // t_prompt, t_prompt,
            jnp.zeros((bp, 8, dm), F32), zc, jnp.zeros((bp, 8, dqk), F32), jnp.zeros((bp, 8, 128), F32),
            convw, convb, wq, wk, gbias, hnorm)
        m0_s = jnp.broadcast_to(_pad_rows8(state_mlstm_m[l][:, :, None]), (bs, 8, 128))
        hm_s, c_s, nn_s, m_s = _mlstm(
            proj, gates, n_p, bs, 1, ls,
            _pad_rows8(cache_mlstm_conv[l], at_end=True), state_mlstm_C[l],
            _pad_rows8(state_mlstm_n[l]), m0_s, convw, convb, wq, wk, gbias, hnorm)
        hm = jnp.concatenate([hm_p, hm_s], axis=0)
        keep = CONV_W - 1
        conv_p = proj[n_p - keep:n_p, :dm].reshape(bp, keep, dm)
        conv_s = proj[n_p:].reshape(bs, ls, -1)[:, ls - keep:, :dm]

        ys, hp_re, hp_im, hs_re, hs_im = _s5(
            proj, 3 * dm, s5_a_re[l], s5_a_im[l], s5_log_dt[l], s5_b_re[l], s5_b_im[l],
            s5_c_re[l], s5_c_im[l], s5_d[l],
            state_s5_re[l].reshape(bs, groups * st), state_s5_im[l].reshape(bs, groups * st),
            n_prompt_rows=n_p)
        ys = _glu(ys, w_glu_b, s5_b_glu[l])

        assert dm == ys.shape[1]
        mix = _mm([(hm, w_out_b, 0), (ys, w_out_b, 1)])
        x, xn = _resnorm(x, mix, mix_norm_post[l], ff2_norm_pre[l], alpha=1.0)
        y = _ffn(xn, wg2, wu2, wd2)[0]
        x = _resnorm(x, y, ff2_norm_post[l], alpha=0.5, two_out=last, n_prompt=n_p)
        x = tuple(x) if last else (x[0][:n_p], x[0][n_p:])

        new_p.append((c_p, nn_p[:, :heads], m_p[:, :heads, 0], conv_p,
                      hp_re[:1].reshape(bp, groups, st), hp_im[:1].reshape(bp, groups, st)))
        new_s.append((c_s, nn_s[:, :heads], m_s[:, :heads, 0], conv_s,
                      hs_re.reshape(bs, groups, st), hs_im.reshape(bs, groups, st)))

    outs_p = [jnp.stack([e[i] for e in new_p]) for i in range(6)]
    outs_s = [jnp.stack([e[i] for e in new_s]) for i in range(6)]
    y_prompt = x[0].reshape(bp, lp, d)
    y_sample = x[1].reshape(bs, ls, d)
    return (y_prompt, y_sample, *outs_p, *outs_s)
```

```python
import functools

import jax
import jax.numpy as jnp
from jax import lax
from jax.experimental import pallas as pl
from jax.experimental.pallas import tpu as pltpu

F32 = jnp.float32
BF16 = jnp.bfloat16
RMS_EPS = 1e-6
CONV_W = 4
S5_WIN = 16
S5_GROUP = 16
S5_STATE = 64
S5_TILE = 128
VMEM_LIMIT = 56 * 1024 * 1024


def _cparams(n_axes):
    return pltpu.CompilerParams(dimension_semantics=("arbitrary",) * n_axes,
                                vmem_limit_bytes=VMEM_LIMIT)


def _rms(v, g):
    return v * lax.rsqrt(jnp.mean(v * v, axis=-1, keepdims=True) + RMS_EPS) * g


def _resnorm_kernel(*refs, alpha, has_y, want_x, want_n, two_in, two_out, n_ptiles, n_ytiles):
    refs = list(refs)
    i = pl.program_id(0)
    if two_in:
        xp_ref = refs.pop(0)
        xs_ref = refs.pop(0)
        x = jnp.where(i < n_ptiles, xp_ref[...], xs_ref[...])
    else:
        x = refs.pop(0)[...]
    if has_y:
        if n_ytiles:
            ya_ref = refs.pop(0)
            yb_ref = refs.pop(0)
            y = jnp.where(i < n_ytiles, ya_ref[...], yb_ref[...])
        else:
            y = refs.pop(0)[...]
        gpost_ref = refs.pop(0)
        x = x + alpha * _rms(y, gpost_ref[...])
    if want_n:
        gpre_ref = refs.pop(0)
    if want_x and two_out:
        op_ref = refs.pop(0)
        os_ref = refs.pop(0)

        @pl.when(i < n_ptiles)
        def _():
            op_ref[...] = x

        @pl.when(i >= n_ptiles)
        def _():
            os_ref[...] = x
    elif want_x:
        refs.pop(0)[...] = x
    if want_n:
        refs.pop(0)[...] = _rms(x, gpre_ref[...]).astype(BF16)


def _resnorm(x, y=None, g_post=None, g_pre=None, *, alpha=1.0, want_x=True, two_out=False, n_prompt=0, tr=256):
    two_in = isinstance(x, tuple)
    d = x[0].shape[1] if two_in else x.shape[1]
    rows = (x[0].shape[0] + x[1].shape[0]) if two_in else x.shape[0]
    n_ptiles = n_prompt // tr
    if two_in or two_out:
        assert n_prompt % tr == 0 and rows - n_prompt == tr
    has_y = y is not None
    want_n = g_pre is not None
    row_spec = pl.BlockSpec((tr, d), lambda i: (i, 0))
    vec_spec = pl.BlockSpec((1, d), lambda i: (0, 0))
    prompt_spec = pl.BlockSpec((tr, d), lambda i: (jnp.minimum(i, n_ptiles - 1), 0))
    sample_spec = pl.BlockSpec((tr, d), lambda i: (0, 0))
    if two_in:
        args, in_specs = [x[0], x[1]], [prompt_spec, sample_spec]
    else:
        args, in_specs = [x], [row_spec]
    n_ytiles = 0
    if has_y and isinstance(y, tuple):
        n_ytiles = y[0].shape[0] // tr
        assert y[0].shape[0] % tr == 0 and y[0].shape[0] + y[1].shape[0] == rows
        args += [y[0], y[1], g_post.reshape(1, d)]
        in_specs += [pl.BlockSpec((tr, d), lambda i: (jnp.minimum(i, n_ytiles - 1), 0)),
                     pl.BlockSpec((tr, d), lambda i: (jnp.maximum(i - n_ytiles, 0), 0)), vec_spec]
    elif has_y:
        args += [y, g_post.reshape(1, d)]
        in_specs += [row_spec, vec_spec]
    if want_n:
        args.append(g_pre.reshape(1, d))
        in_specs.append(vec_spec)
    out_shape, out_specs = [], []
    if want_x and two_out:
        out_shape += [jax.ShapeDtypeStruct((n_prompt, d), F32), jax.ShapeDtypeStruct((tr, d), F32)]
        out_specs += [prompt_spec, sample_spec]
    elif want_x:
        out_shape.append(jax.ShapeDtypeStruct((rows, d), F32))
        out_specs.append(row_spec)
    if want_n:
        out_shape.append(jax.ShapeDtypeStruct((rows, d), BF16))
        out_specs.append(row_spec)
    return pl.pallas_call(
        functools.partial(_resnorm_kernel, alpha=alpha, has_y=has_y, want_x=want_x, want_n=want_n,
                          two_in=two_in, two_out=two_out, n_ptiles=n_ptiles, n_ytiles=n_ytiles),
        grid=(rows // tr,), in_specs=in_specs, out_specs=out_specs, out_shape=out_shape,
        compiler_params=_cparams(1), name="resnorm")(*args)


def _ffn_kernel(*refs, n_side, side_steps, emit_bf16):
    xn_ref, wg_ref, wu_ref, wd_ref = refs[:4]
    side_in = refs[4:4 + n_side]
    o_ref = refs[4 + n_side]
    n_emit = 3 if emit_bf16 else 0
    emit_out = refs[5 + n_side:5 + n_side + n_emit]
    side_out = refs[5 + n_side + n_emit:]
    j = pl.program_id(1)

    @pl.when(j == 0)
    def _():
        o_ref[...] = jnp.zeros_like(o_ref)

    wg = wg_ref[...].astype(BF16)
    wu = wu_ref[...].astype(BF16)
    wd = wd_ref[...].astype(BF16)
    for dst, w in zip(emit_out, (wg, wu, wd)):
        dst[...] = w
    xn = xn_ref[...]
    g = jnp.dot(xn, wg, preferred_element_type=F32)
    u = jnp.dot(xn, wu, preferred_element_type=F32)
    h = (g * jax.nn.sigmoid(g) * u).astype(BF16)
    o_ref[...] += jnp.dot(h, wd, preferred_element_type=F32)

    if n_side:
        @pl.when(pl.program_id(0) * pl.num_programs(1) + j < side_steps)
        def _():
            for src, dst in zip(side_in, side_out):
                dst[...] = src[...].astype(BF16)


def _ffn(xn, wg, wu, wd, side=(), *, tile0=0, n_tiles=None, emit_bf16=False, tm=768, tf=256):
    d = xn.shape[1]
    f = wg.shape[1]
    nf = f // tf
    n_tiles = xn.shape[0] // tm - tile0 if n_tiles is None else n_tiles
    rows = n_tiles * tm
    n_steps = n_tiles * nf
    side_args, side_in_specs, side_out_specs, side_out_shape = [], [], [], []
    side_steps = max([nblk for _, _, nblk, _ in side], default=0)
    for w, rb, nblk, cols in side:
        assert rb * nblk <= w.shape[0] and nblk <= n_steps and rb % 16 == 0 and cols % 128 == 0
        spec = pl.BlockSpec((rb, cols), lambda i, j, nblk=nblk: (jnp.minimum(i * nf + j, nblk - 1), 0))
        side_args.append(w)
        side_in_specs.append(spec)
        side_out_specs.append(spec)
        side_out_shape.append(jax.ShapeDtypeStruct((rb * nblk, cols), BF16))
    w_specs = [pl.BlockSpec((d, tf), lambda i, j: (0, j)),
               pl.BlockSpec((d, tf), lambda i, j: (0, j)),
               pl.BlockSpec((tf, d), lambda i, j: (j, 0))]
    emit_specs = w_specs if emit_bf16 else []
    emit_shape = [jax.ShapeDtypeStruct(w.shape, BF16) for w in (wg, wu, wd)] if emit_bf16 else []
    outs = pl.pallas_call(
        functools.partial(_ffn_kernel, n_side=len(side), side_steps=side_steps, emit_bf16=emit_bf16),
        grid=(n_tiles, nf),
        in_specs=[pl.BlockSpec((tm, d), lambda i, j: (tile0 + i, 0), pipeline_mode=pl.Buffered(1))]
        + w_specs + side_in_specs,
        out_specs=[pl.BlockSpec((tm, d), lambda i, j: (i, 0), pipeline_mode=pl.Buffered(1))]
        + emit_specs + side_out_specs,
        out_shape=[jax.ShapeDtypeStruct((rows, d), F32)] + emit_shape + side_out_shape,
        compiler_params=_cparams(2), name="ffn")(xn, wg, wu, wd, *side_args)
    n_emit = len(emit_shape)
    return outs[0], outs[1:1 + n_emit], outs[1 + n_emit:]


def _mm_kernel(*refs, n_pairs, trans_w):
    o_ref = refs[-1]
    acc = None
    w_dim = 1 if trans_w else 0
    for p in range(n_pairs):
        d = lax.dot_general(refs[2 * p][...], refs[2 * p + 1][...].astype(BF16),
                            (((1,), (w_dim,)), ((), ())), preferred_element_type=F32)
        acc = d if acc is None else acc + d
    o_ref[...] = acc.astype(o_ref.dtype)


def _mm(pairs, *, trans_w=False, tm=768, tn=512, out_dtype=F32):
    rows = pairs[0][0].shape[0]
    n = pairs[0][1].shape[0 if trans_w else 1]
    tn = min(tn, n)
    args, in_specs = [], []
    for a, w, kb in pairs:
        k = a.shape[1]
        args += [a, w]
        in_specs.append(pl.BlockSpec((tm, k), lambda i, j: (i, 0)))
        if trans_w:
            in_specs.append(pl.BlockSpec((tn, k), lambda i, j, kb=kb: (j, kb)))
        else:
            in_specs.append(pl.BlockSpec((k, tn), lambda i, j, kb=kb: (kb, j)))
    return pl.pallas_call(
        functools.partial(_mm_kernel, n_pairs=len(pairs), trans_w=trans_w),
        grid=(rows // tm, n // tn),
        in_specs=in_specs,
        out_specs=pl.BlockSpec((tm, tn), lambda i, j: (i, j)),
        out_shape=jax.ShapeDtypeStruct((rows, n), out_dtype),
        compiler_params=_cparams(2), name="mm")(*args)


def _mm_resnorm_kernel(*refs, n_pairs, tn):
    x_ref, gpost_ref, gpre_ref, xo_ref, xn_ref = refs[2 * n_pairs:]
    j = pl.program_id(1)
    acc = None
    for p in range(n_pairs):
        d = jnp.dot(refs[2 * p][...], refs[2 * p + 1][...], preferred_element_type=F32)
        acc = d if acc is None else acc + d
    xo_ref[:, pl.ds(pl.multiple_of(j * tn, tn), tn)] = acc

    @pl.when(j == pl.num_programs(1) - 1)
    def _():
        x = x_ref[...] + _rms(xo_ref[...], gpost_ref[...])
        xo_ref[...] = x
        xn_ref[...] = _rms(x, gpre_ref[...]).astype(BF16)


def _mm_resnorm(pairs, x, g_post, g_pre, *, tm=768, tn=256):
    rows, d = x.shape
    args, in_specs = [], []
    for a, w, kb in pairs:
        k = a.shape[1]
        args += [a, w]
        in_specs += [pl.BlockSpec((tm, k), lambda i, j: (i, 0), pipeline_mode=pl.Buffered(1)),
                     pl.BlockSpec((k, tn), lambda i, j, kb=kb: (kb, j))]
    row_once = lambda: pl.BlockSpec((tm, d), lambda i, j: (i, 0), pipeline_mode=pl.Buffered(1))
    vec = pl.BlockSpec((1, d), lambda i, j: (0, 0))
    return pl.pallas_call(
        functools.partial(_mm_resnorm_kernel, n_pairs=len(pairs), tn=tn),
        grid=(rows // tm, d // tn),
        in_specs=in_specs + [row_once(), vec, vec],
        out_specs=[row_once(), row_once()],
        out_shape=[jax.ShapeDtypeStruct((rows, d), F32), jax.ShapeDtypeStruct((rows, d), BF16)],
        compiler_params=_cparams(2), name="mm_resnorm")(
            *args, x, g_post.reshape(1, d), g_pre.reshape(1, d))


def _cumsum_rows(x):
    t = x.shape[0]
    row = lax.broadcasted_iota(jnp.int32, x.shape, 0)
    k = 1
    while k < t:
        x = x + jnp.where(row >= k, pltpu.roll(x, k, axis=0), 0.0)
        k *= 2
    return x


def _mlstm_kernel(um_ref, v_ref, o_ref, gt_ref, cbuf_ref, c0_ref, n0_ref, m0_ref,
                  convw_ref, convb_ref, wq_ref, wk_ref, gbias_ref, hn_ref,
                  hm_ref, cout_ref, nout_ref, mout_ref,
                  xp_s, c_s, n_s, m_s, *, t, heads, dqk, dv, n_chunks):
    c = pl.program_id(1)

    @pl.when(c == 0)
    def _():
        xp_s[0:8, :] = cbuf_ref[0]
        c_s[...] = c0_ref[0]
        n_s[...] = n0_ref[0]
        m_s[...] = m0_ref[0]

    xp_s[8:8 + t, :] = um_ref[...]
    cv = convb_ref[...] + xp_s[8 - (CONV_W - 1):8 - (CONV_W - 1) + t, :] * convw_ref[0:1, :]
    for j in range(1, CONV_W):
        off = 8 - (CONV_W - 1) + j
        cv = cv + xp_s[off:off + t, :] * convw_ref[j:j + 1, :]
    cv = cv * jax.nn.sigmoid(cv)
    xp_s[0:8, :] = xp_s[t:t + 8, :]

    gt = gt_ref[...] + gbias_ref[...]
    lane = lax.broadcasted_iota(jnp.int32, gt.shape, 1)
    lf = jnp.minimum(gt, 0.0) - jnp.log1p(jnp.exp(-jnp.abs(gt)))
    g_all = jnp.where(lane < heads, gt, lf)
    b_all = _cumsum_rows(g_all)
    mix = jnp.where(lane < heads, g_all, b_all)
    tp = max(t, 128)
    if tp != t:
        mix_p = jnp.concatenate([mix, jnp.zeros((tp - t, mix.shape[1]), F32)], axis=0)
    else:
        mix_p = mix
    mix_t = mix_p.T[:, :t]

    rr = lax.broadcasted_iota(jnp.int32, (t, t), 0)
    cc = lax.broadcasted_iota(jnp.int32, (t, t), 1)
    causal = cc <= rr

    for h in range(heads):
        ig_c = mix[:, h:h + 1]
        b_c = mix[:, heads + h:heads + h + 1]
        ig_r = mix_t[h:h + 1, :]
        b_r = mix_t[heads + h:heads + h + 1, :]
        m0 = m_s[h:h + 1, 0:1]
        log_d = jnp.where(causal, b_c - b_r + ig_r, -jnp.inf)
        inter = b_c + m0
        m_t = jnp.maximum(inter, jnp.max(log_d, axis=1, keepdims=True))
        dmat = jnp.exp(log_d - m_t)
        cvh = cv[:, h * dv:(h + 1) * dv].astype(BF16)
        q = jnp.dot(cvh, wq_ref[h], preferred_element_type=F32)
        k = jnp.dot(cvh, wk_ref[h], preferred_element_type=F32) * (dqk ** -0.5)
        qb = q.astype(BF16)
        kb = k.astype(BF16)
        s = lax.dot_general(qb, kb, (((1,), (1,)), ((), ())), preferred_element_type=F32) * dmat
        w_inter = jnp.exp(inter - m_t)
        c0 = c_s[h]
        n0 = n_s[h:h + 1, :]
        vh = v_ref[:, h * dv:(h + 1) * dv]
        num = (w_inter * jnp.dot(qb, c0.astype(BF16), preferred_element_type=F32)
               + jnp.dot(s.astype(BF16), vh.astype(BF16), preferred_element_type=F32))
        den = w_inter * jnp.sum(q * n0, axis=1, keepdims=True) + jnp.sum(s, axis=1, keepdims=True)
        hh = num / jnp.maximum(jnp.abs(den), jnp.exp(-m_t))
        b_last = b_c[t - 1:t, :]
        log_w = b_last - b_c + ig_c
        m_new = jnp.maximum(b_last + m0, jnp.max(log_w, axis=0, keepdims=True))
        w_col = jnp.exp(log_w - m_new)
        decay = jnp.exp(b_last + m0 - m_new)
        wv = (w_col * vh).astype(BF16)
        c_s[h] = decay * c0 + lax.dot_general(kb, wv, (((0,), (0,)), ((), ())),
                                              preferred_element_type=F32)
        n_s[h:h + 1, :] = decay * n0 + jnp.sum(w_col * k, axis=0, keepdims=True)
        m_s[h:h + 1, :] = jnp.broadcast_to(m_new, (1, m_s.shape[1]))
        hn = _rms(hh, hn_ref[h:h + 1, :])
        og = jax.nn.sigmoid(o_ref[:, h * dv:(h + 1) * dv])
        hm_ref[:, h * dv:(h + 1) * dv] = (hn * og).astype(BF16)

    @pl.when(c == n_chunks - 1)
    def _():
        cout_ref[0] = c_s[...]
        nout_ref[0] = n_s[...]
        mout_ref[0] = m_s[...]


def _mlstm(p, gates, row0, n_streams, n_chunks, t, cbuf, c0, n0, m0,
           convw, convb, wq, wk, gbias, hnorm):
    heads, dv, dqk = wq.shape
    dm = heads * dv
    blk0 = row0 // t
    assert row0 % t == 0

    def rows(b, c):
        return blk0 + b * n_chunks + c

    kern = functools.partial(_mlstm_kernel, t=t, heads=heads, dqk=dqk, dv=dv, n_chunks=n_chunks)
    n_rows = n_streams * n_chunks * t
    outs = pl.pallas_call(
        kern,
        grid=(n_streams, n_chunks),
        in_specs=[pl.BlockSpec((t, dm), lambda b, c: (rows(b, c), 0)),
                  pl.BlockSpec((t, dm), lambda b, c: (rows(b, c), 1)),
                  pl.BlockSpec((t, dm), lambda b, c: (rows(b, c), 2)),
                  pl.BlockSpec((t, 128), lambda b, c: (rows(b, c), 0)),
                  pl.BlockSpec((1, 8, dm), lambda b, c: (b, 0, 0)),
                  pl.BlockSpec((1, heads, dqk, dv), lambda b, c: (b, 0, 0, 0)),
                  pl.BlockSpec((1, 8, dqk), lambda b, c: (b, 0, 0)),
                  pl.BlockSpec((1, 8, 128), lambda b, c: (b, 0, 0)),
                  pl.BlockSpec((CONV_W, dm), lambda b, c: (0, 0)),
                  pl.BlockSpec((1, dm), lambda b, c: (0, 0)),
                  pl.BlockSpec((heads, dv, dqk), lambda b, c: (0, 0, 0)),
                  pl.BlockSpec((heads, dv, dqk), lambda b, c: (0, 0, 0)),
                  pl.BlockSpec((1, 128), lambda b, c: (0, 0)),
                  pl.BlockSpec((heads, dv), lambda b, c: (0, 0))],
        out_specs=[pl.BlockSpec((t, dm), lambda b, c: (b * n_chunks + c, 0)),
                   pl.BlockSpec((1, heads, dqk, dv), lambda b, c: (b, 0, 0, 0)),
                   pl.BlockSpec((1, 8, dqk), lambda b, c: (b, 0, 0)),
                   pl.BlockSpec((1, 8, 128), lambda b, c: (b, 0, 0))],
        out_shape=[jax.ShapeDtypeStruct((n_rows, dm), BF16),
                   jax.ShapeDtypeStruct((n_streams, heads, dqk, dv), F32),
                   jax.ShapeDtypeStruct((n_streams, 8, dqk), F32),
                   jax.ShapeDtypeStruct((n_streams, 8, 128), F32)],
        scratch_shapes=[pltpu.VMEM((t + 8, dm), F32),
                        pltpu.VMEM((heads, dqk, dv), F32),
                        pltpu.VMEM((8, dqk), F32),
                        pltpu.VMEM((8, 128), F32)],
        compiler_params=_cparams(2), name="mlstm")(
            p, p, p, gates, cbuf, c0, n0, m0, convw, convb, wq, wk, gbias, hnorm)
    return outs


def _s5_kernel(u_ref, ax_re_ref, ax_im_ref, ldtx_ref, bt_re_ref, bt_im_ref, ct_re_ref, ct_im_ref,
               arow_re_ref, arow_im_ref, ldtrow_ref, d_ref, h0_re_ref, h0_im_ref,
               y_ref, hp_re_ref, hp_im_ref, hs_re_ref, hs_im_ref,
               h_re_s, h_im_s, bm_re_s, bm_im_s, cd_re_s, cd_im_s, *, n_prompt, n_win):
    win, grp, st = S5_WIN, S5_GROUP, S5_STATE
    nc = u_ref.shape[1]
    nl = h_re_s.shape[1]

    dt = jnp.exp(ldtx_ref[...])
    ar = ax_re_ref[...]
    ai = ax_im_ref[...]
    mag = jnp.exp(ar * dt)
    ab_re = mag * jnp.cos(ai * dt)
    ab_im = mag * jnp.sin(ai * dt)
    den = ar * ar + ai * ai
    inv_re = ar / den
    inv_im = -ai / den
    f_re = ab_re - 1.0
    f_im = ab_im
    z_re = f_re * inv_re - f_im * inv_im
    z_im = f_re * inv_im + f_im * inv_re
    bb_re = z_re * bt_re_ref[...] - z_im * bt_im_ref[...]
    bb_im = z_re * bt_im_ref[...] + z_im * bt_re_ref[...]
    own_b = (lax.broadcasted_iota(jnp.int32, (nc, nl), 0) // grp
             == lax.broadcasted_iota(jnp.int32, (nc, nl), 1) // st)
    reps_b = nl // bb_re.shape[1]
    bd_re = jnp.where(own_b, jnp.concatenate([bb_re] * reps_b, axis=1), 0.0)
    bd_im = jnp.where(own_b, jnp.concatenate([bb_im] * reps_b, axis=1), 0.0)
    own_c =(lax.broadcasted_iota(jnp.int32, (nl, nc), 0) // st
             == lax.broadcasted_iota(jnp.int32, (nl, nc), 1) // grp)
    reps_c = nc // ct_re_ref.shape[1]
    cd_re_s[...] = jnp.where(own_c, jnp.concatenate([ct_re_ref[...]] * reps_c, axis=1), 0.0).astype(BF16)
    cd_im_s[...] = jnp.where(own_c, -jnp.concatenate([ct_im_ref[...]] * reps_c, axis=1), 0.0).astype(BF16)

    dtr = jnp.exp(ldtrow_ref[...])
    magr = jnp.exp(arow_re_ref[...] * dtr)
    a_re = magr * jnp.cos(arow_im_ref[...] * dtr)
    a_im = magr * jnp.sin(arow_im_ref[...] * dtr)
    aw_re, aw_im = a_re, a_im
    for _ in range(win.bit_length() - 1):
        aw_re, aw_im = aw_re * aw_re - aw_im * aw_im, 2.0 * aw_re * aw_im

    pw_re = jnp.ones_like(a_re)
    pw_im = jnp.zeros_like(a_im)
    for s in range(win - 1, -1, -1):
        bm_re_s[s * nc:(s + 1) * nc, :] = (bd_re * pw_re - bd_im * pw_im).astype(BF16)
        bm_im_s[s * nc:(s + 1) * nc, :] = (bd_re * pw_im + bd_im * pw_re).astype(BF16)
        pw_re, pw_im = pw_re * a_re - pw_im * a_im, pw_re * a_im + pw_im * a_re

    def slab(s):
        return u_ref[pl.ds(s, n_win, stride=win), :]

    u_all = jnp.concatenate([slab(s).astype(BF16) for s in range(win)], axis=1)
    h_re_s[...] = jnp.dot(u_all, bm_re_s[...], preferred_element_type=F32)
    h_im_s[...] = jnp.dot(u_all, bm_im_s[...], preferred_element_type=F32)

    h0r = h0_re_ref[...]
    h0i = h0_im_ref[...]
    hs_re_ref[...] = aw_re * h0r - aw_im * h0i + h_re_s[n_prompt:n_win, :]
    hs_im_ref[...] = aw_re * h0i + aw_im * h0r + h_im_s[n_prompt:n_win, :]
    h_re_s[n_prompt:n_win, :] = h0r
    h_im_s[n_prompt:n_win, :] = h0i

    def carry_step(c, carry):
        hr, hi = carry
        lr = h_re_s[pl.ds(c, 1), :]
        li = h_im_s[pl.ds(c, 1), :]
        h_re_s[pl.ds(c, 1), :] = hr
        h_im_s[pl.ds(c, 1), :] = hi
        return aw_re * hr - aw_im * hi + lr, aw_re * hi + aw_im * hr + li

    zero = jnp.zeros((1, nl), F32)
    hr, hi = lax.fori_loop(0, n_prompt, carry_step, (zero, zero))
    hp_re_ref[...] = jnp.broadcast_to(hr, hp_re_ref.shape)
    hp_im_ref[...] = jnp.broadcast_to(hi, hp_im_ref.shape)

    def step2(t, carry):
        xt = slab(t)
        xb = xt.astype(BF16)
        hr = h_re_s[...]
        hi = h_im_s[...]
        nr = a_re * hr - a_im * hi + jnp.dot(xb, bm_re_s[(win - 1) * nc:, :], preferred_element_type=F32)
        ni = a_re * hi + a_im * hr + jnp.dot(xb, bm_im_s[(win - 1) * nc:, :], preferred_element_type=F32)
        h_re_s[...] = nr
        h_im_s[...] = ni
        y = (jnp.dot(nr.astype(BF16), cd_re_s[...], preferred_element_type=F32)
             + jnp.dot(ni.astype(BF16), cd_im_s[...], preferred_element_type=F32)
             + d_ref[...] * xt)
        y_ref[pl.ds(t, n_win, stride=win), :] = y
        return carry

    lax.fori_loop(0, win, step2, 0, unroll=2)


def _s5(proj, col0, a_re, a_im, log_dt, b_re, b_im, c_re, c_im, d, h0_re, h0_im, *, n_prompt_rows):
    groups, st = a_re.shape
    grp, win, nc = S5_GROUP, S5_WIN, S5_TILE
    gt = nc // grp
    nl = gt * st
    tiles = groups // gt
    rows = proj.shape[0]
    n_win = rows // win
    n_prompt = n_prompt_rows // win
    n_s = n_win - n_prompt
    assert col0 % nc == 0 and (groups * grp) % nc == 0 and win & (win - 1) == 0

    dup = lambda x: jnp.concatenate([x, x], axis=-1)
    rep = lambda x: jnp.repeat(x, grp, axis=0)
    ax_re = rep(dup(a_re))
    ax_im = rep(dup(a_im))
    ldtx = jnp.broadcast_to(rep(log_dt[:, None]), (groups * grp, 2 * st))
    bt_re = dup(jnp.swapaxes(b_re, 1, 2)).reshape(groups * grp, 2 * st)
    bt_im = dup(jnp.swapaxes(b_im, 1, 2)).reshape(groups * grp, 2 * st)
    ct_re = jnp.tile(jnp.swapaxes(c_re, 1, 2), (1, 1, 128 // grp)).reshape(groups * st, 128)
    ct_im = jnp.tile(jnp.swapaxes(c_im, 1, 2), (1, 1, 128 // grp)).reshape(groups * st, 128)
    arow_re = a_re.reshape(tiles, 1, nl)
    arow_im = a_im.reshape(tiles, 1, nl)
    ldtrow = jnp.repeat(log_dt, st).reshape(tiles, 1, nl)
    drow = d.reshape(tiles, 1, nc)

    rowblk = lambda w: pl.BlockSpec((nc, w), lambda j: (j, 0))
    vec = lambda w: pl.BlockSpec((None, 1, w), lambda j: (j, 0, 0))
    lanes = groups * st
    kern = functools.partial(_s5_kernel, n_prompt=n_prompt, n_win=n_win)
    return pl.pallas_call(
        kern, grid=(tiles,),
        in_specs=[pl.BlockSpec((rows, nc), lambda j: (0, col0 // nc + j)),
                  rowblk(2 * st), rowblk(2 * st), rowblk(2 * st), rowblk(2 * st), rowblk(2 * st),
                  pl.BlockSpec((nl, 128), lambda j: (j, 0)), pl.BlockSpec((nl, 128), lambda j: (j, 0)),
                  vec(nl), vec(nl), vec(nl), vec(nc),
                  pl.BlockSpec((n_s, nl), lambda j: (0, j)), pl.BlockSpec((n_s, nl), lambda j: (0, j))],
        out_specs=[pl.BlockSpec((rows, nc), lambda j: (0, j)),
                   pl.BlockSpec((8, nl), lambda j: (0, j)), pl.BlockSpec((8, nl), lambda j: (0, j)),
                   pl.BlockSpec((n_s, nl), lambda j: (0, j)), pl.BlockSpec((n_s, nl), lambda j: (0, j))],
        out_shape=[jax.ShapeDtypeStruct((rows, groups * grp), F32),
                   jax.ShapeDtypeStruct((8, lanes), F32), jax.ShapeDtypeStruct((8, lanes), F32),
                   jax.ShapeDtypeStruct((n_s, lanes), F32), jax.ShapeDtypeStruct((n_s, lanes), F32)],
        scratch_shapes=[pltpu.VMEM((n_win, nl), F32), pltpu.VMEM((n_win, nl), F32),
                        pltpu.VMEM((win * nc, nl), BF16), pltpu.VMEM((win * nc, nl), BF16),
                        pltpu.VMEM((nl, nc), BF16), pltpu.VMEM((nl, nc), BF16)],
        compiler_params=_cparams(1), name="s5")(
            proj, ax_re, ax_im, ldtx, bt_re, bt_im, ct_re, ct_im, arow_re, arow_im, ldtrow, drow,
            h0_re, h0_im)


def _glu_kernel(y_ref, w_ref, b_ref, o_ref):
    g = jax.nn.gelu(y_ref[...])
    z = jnp.dot(g.astype(BF16), w_ref[...], preferred_element_type=F32) + b_ref[...]
    o_ref[...] = (g * jax.nn.sigmoid(z)).astype(BF16)


def _glu(y, w, b, *, tm=768):
    rows, d = y.shape
    return pl.pallas_call(
        _glu_kernel, grid=(rows // tm,),
        in_specs=[pl.BlockSpec((tm, d), lambda i: (i, 0)),
                  pl.BlockSpec((d, d), lambda i: (0, 0)),
                  pl.BlockSpec((1, d), lambda i: (0, 0))],
        out_specs=pl.BlockSpec((tm, d), lambda i: (i, 0)),
        out_shape=jax.ShapeDtypeStruct((rows, d), BF16),
        compiler_params=_cparams(1), name="glu")(y, w, b.reshape(1, d))


def _pad_rows8(x, at_end=False):
    b, r, n = x.shape
    z = jnp.zeros((b, 8 - r, n), x.dtype)
    return jnp.concatenate([x, z] if not at_end else [z, x], axis=1)


def kernel(x_prompt, x_sample, state_mlstm_C, state_mlstm_n, state_mlstm_m, cache_mlstm_conv,
           state_s5_re, state_s5_im, ff1_norm_pre, ff1_norm_post, ff1_w_gate, ff1_w_up, ff1_w_down,
           mix_norm_pre, w_in, mlstm_conv_w, mlstm_conv_b, mlstm_w_q, mlstm_w_k, mlstm_b_i, mlstm_b_f,
           mlstm_head_norm, s5_a_re, s5_a_im, s5_log_dt, s5_b_re, s5_b_im, s5_c_re, s5_c_im, s5_d,
           s5_w_glu, s5_b_glu, w_out, mix_norm_post, ff2_norm_pre, ff2_norm_post, ff2_w_gate, ff2_w_up,
           ff2_w_down):
    depth = w_in.shape[0]
    bp, lp, d = x_prompt.shape
    bs, ls, _ = x_sample.shape
    assert bp == 1 and ls == S5_WIN and lp % 256 == 0
    heads, dv, dqk = mlstm_w_q.shape[1:]
    dm = heads * dv
    groups, st = s5_a_re.shape[1:]
    n_p = bp * lp
    n_s = bs * ls
    t_prompt = 256

    x = (x_prompt.reshape(n_p, d), x_sample.reshape(n_s, d))
    new_p, new_s = [], []
    for l in range(depth):
        bf = lambda w: w[l].astype(BF16)
        last = l == depth - 1
        n_main = 3 * dm + groups * S5_GROUP
        xn = _resnorm(x, g_pre=ff1_norm_pre[l], want_x=False, n_prompt=n_p)[0]
        w_in_t = jnp.swapaxes(w_in[l], 0, 1)
        d_ff = ff2_w_gate.shape[2]
        later = [(ff2_w_gate[l], 16, d // 16, d_ff), (ff2_w_up[l], 16, d // 16, d_ff),
                 (ff2_w_down[l], 64, d_ff // 64, d), (w_in_t, 32, n_main // 32, d),
                 (w_out[l], 16, d // 16, d), (s5_w_glu[l], 16, s5_w_glu.shape[1] // 16, s5_w_glu.shape[2])]
        y_a, (wg1, wu1, wd1), _ = _ffn(xn, ff1_w_gate[l], ff1_w_up[l], ff1_w_down[l],
                                       n_tiles=1, emit_bf16=True)
        y_b, _, (wg2, wu2, wd2, w_in_b, w_out_b, w_glu_b) = _ffn(xn, wg1, wu1, wd1, side=later, tile0=1)
        x, hmix = _resnorm(x, (y_a, y_b), ff1_norm_post[l], mix_norm_pre[l], alpha=0.5, n_prompt=n_p)

        proj = _mm([(hmix, w_in_b, 0)], trans_w=True)
        w_gate_rows = jnp.pad(w_in_t[n_main:], ((0, 128 - 2 * heads), (0, 0)))
        gates = _mm([(hmix, w_gate_rows, 0)], trans_w=True)
        gbias = jnp.pad(jnp.concatenate([mlstm_b_i[l], mlstm_b_f[l]]), (0, 128 - 2 * heads)).reshape(1, 128)

        wq = bf(mlstm_w_q)
        wk = bf(mlstm_w_k)
        convw = mlstm_conv_w[l]
        convb = mlstm_conv_b[l].reshape(1, dm)
        hnorm = mlstm_head_norm[l]
        zc = jnp.zeros((bp, heads, dqk, dv), F32)
        hm_p, c_p, nn_p, m_p = _mlstm(
            proj, gates, 0, bp, lp // t_prompt, t_prompt,
            jnp.zeros((bp, 8, dm), F32), zc, jnp.zeros((bp, 8, dqk), F32), jnp.zeros((bp, 8, 128), F32),
            convw, convb, wq, wk, gbias, hnorm)
        m0_s = jnp.broadcast_to(_pad_rows8(state_mlstm_m[l][:, :, None]), (bs, 8, 128))
        hm_s, c_s, nn_s, m_s = _mlstm(
            proj, gates, n_p, bs, 1, ls,
            _pad_rows8(cache_mlstm_conv[l], at_end=True), state_mlstm_C[l],
            _pad_rows8(state_mlstm_n[l]), m0_s, convw, convb, wq, wk, gbias, hnorm)
        hm = jnp.concatenate([hm_p, hm_s], axis=0)
        keep = CONV_W - 1
        conv_p = proj[n_p - keep:n_p, :dm].reshape(bp, keep, dm)
        conv_s = proj[n_p:].reshape(bs, ls, -1)[:, ls - keep:, :dm]

        ys, hp_re, hp_im, hs_re, hs_im = _s5(
            proj, 3 * dm, s5_a_re[l], s5_a_im[l], s5_log_dt[l], s5_b_re[l], s5_b_im[l],
            s5_c_re[l], s5_c_im[l], s5_d[l],
            state_s5_re[l].reshape(bs, groups * st), state_s5_im[l].reshape(bs, groups * st),
            n_prompt_rows=n_p)
        ys = _glu(ys, w_glu_b, s5_b_glu[l])

        assert dm == ys.shape[1]
        x, xn = _mm_resnorm([(hm, w_out_b, 0), (ys, w_out_b, 1)], x, mix_norm_post[l], ff2_norm_pre[l])
        y = _ffn(xn, wg2, wu2, wd2, tm=1056)[0]
        x = _resnorm(x, y, ff2_norm_post[l], alpha=0.5, two_out=last, n_prompt=n_p)
        x = tuple(x) if last else (x[0][:n_p], x[0][n_p:])

        new_p.append((c_p, nn_p[:, :heads], m_p[:, :heads, 0], conv_p,
                      hp_re[:1].reshape(bp, groups, st), hp_im[:1].reshape(bp, groups, st)))
        new_s.append((c_s, nn_s[:, :heads], m_s[:, :heads, 0], conv_s,
                      hs_re.reshape(bs, groups, st), hs_im.reshape(bs, groups, st)))

    outs_p = [jnp.stack([e[i] for e in new_p]) for i in range(6)]
    outs_s = [jnp.stack([e[i] for e in new_s]) for i in range(6)]
    y_prompt = x[0].reshape(bp, lp, d)
    y_sample = x[1].reshape(bs, ls, d)
    return (y_prompt, y_sample, *outs_p, *outs_s)
```

```python
import functools

import jax
import jax.numpy as jnp
from jax import lax
from jax.experimental import pallas as pl
from jax.experimental.pallas import tpu as pltpu

F32 = jnp.float32
BF16 = jnp.bfloat16
RMS_EPS = 1e-6
CONV_W = 4
S5_WIN = 16
S5_GROUP = 16
S5_STATE = 64
S5_TILE = 128
VMEM_LIMIT = 56 * 1024 * 1024


def _cparams(n_axes):
    return pltpu.CompilerParams(dimension_semantics=("arbitrary",) * n_axes,
                                vmem_limit_bytes=VMEM_LIMIT)


def _rms(v, g):
    return v * lax.rsqrt(jnp.mean(v * v, axis=-1, keepdims=True) + RMS_EPS) * g


def _resnorm_kernel(*refs, alpha, has_y, want_x, want_n, two_in, two_out, n_ptiles, n_ytiles):
    refs = list(refs)
    i = pl.program_id(0)
    if two_in:
        xp_ref = refs.pop(0)
        xs_ref = refs.pop(0)
        x = jnp.where(i < n_ptiles, xp_ref[...], xs_ref[...])
    else:
        x = refs.pop(0)[...]
    if has_y:
        if n_ytiles:
            ya_ref = refs.pop(0)
            yb_ref = refs.pop(0)
            y = jnp.where(i < n_ytiles, ya_ref[...], yb_ref[...])
        else:
            y = refs.pop(0)[...]
        gpost_ref = refs.pop(0)
        x = x + alpha * _rms(y, gpost_ref[...])
    if want_n:
        gpre_ref = refs.pop(0)
    if want_x and two_out:
        op_ref = refs.pop(0)
        os_ref = refs.pop(0)

        @pl.when(i < n_ptiles)
        def _():
            op_ref[...] = x

        @pl.when(i >= n_ptiles)
        def _():
            os_ref[...] = x
    elif want_x:
        refs.pop(0)[...] = x
    if want_n:
        refs.pop(0)[...] = _rms(x, gpre_ref[...]).astype(BF16)


def _resnorm(x, y=None, g_post=None, g_pre=None, *, alpha=1.0, want_x=True, two_out=False, n_prompt=0, tr=256):
    two_in = isinstance(x, tuple)
    d = x[0].shape[1] if two_in else x.shape[1]
    rows = (x[0].shape[0] + x[1].shape[0]) if two_in else x.shape[0]
    n_ptiles = n_prompt // tr
    if two_in or two_out:
        assert n_prompt % tr == 0 and rows - n_prompt == tr
    has_y = y is not None
    want_n = g_pre is not None
    row_spec = pl.BlockSpec((tr, d), lambda i: (i, 0))
    vec_spec = pl.BlockSpec((1, d), lambda i: (0, 0))
    prompt_spec = pl.BlockSpec((tr, d), lambda i: (jnp.minimum(i, n_ptiles - 1), 0))
    sample_spec = pl.BlockSpec((tr, d), lambda i: (0, 0))
    if two_in:
        args, in_specs = [x[0], x[1]], [prompt_spec, sample_spec]
    else:
        args, in_specs = [x], [row_spec]
    n_ytiles = 0
    if has_y and isinstance(y, tuple):
        n_ytiles = y[0].shape[0] // tr
        assert y[0].shape[0] % tr == 0 and y[0].shape[0] + y[1].shape[0] == rows
        args += [y[0], y[1], g_post.reshape(1, d)]
        in_specs += [pl.BlockSpec((tr, d), lambda i: (jnp.minimum(i, n_ytiles - 1), 0)),
                     pl.BlockSpec((tr, d), lambda i: (jnp.maximum(i - n_ytiles, 0), 0)), vec_spec]
    elif has_y:
        args += [y, g_post.reshape(1, d)]
        in_specs += [row_spec, vec_spec]
    if want_n:
        args.append(g_pre.reshape(1, d))
        in_specs.append(vec_spec)
    out_shape, out_specs = [], []
    if want_x and two_out:
        out_shape += [jax.ShapeDtypeStruct((n_prompt, d), F32), jax.ShapeDtypeStruct((tr, d), F32)]
        out_specs += [prompt_spec, sample_spec]
    elif want_x:
        out_shape.append(jax.ShapeDtypeStruct((rows, d), F32))
        out_specs.append(row_spec)
    if want_n:
        out_shape.append(jax.ShapeDtypeStruct((rows, d), BF16))
        out_specs.append(row_spec)
    return pl.pallas_call(
        functools.partial(_resnorm_kernel, alpha=alpha, has_y=has_y, want_x=want_x, want_n=want_n,
                          two_in=two_in, two_out=two_out, n_ptiles=n_ptiles, n_ytiles=n_ytiles),
        grid=(rows // tr,), in_specs=in_specs, out_specs=out_specs, out_shape=out_shape,
        compiler_params=_cparams(1), name="resnorm")(*args)


def _ffn_kernel(*refs, n_side, side_steps, emit_bf16):
    xn_ref, wg_ref, wu_ref, wd_ref = refs[:4]
    side_in = refs[4:4 + n_side]
    o_ref = refs[4 + n_side]
    n_emit = 3 if emit_bf16 else 0
    emit_out = refs[5 + n_side:5 + n_side + n_emit]
    side_out = refs[5 + n_side + n_emit:]
    j = pl.program_id(1)

    @pl.when(j == 0)
    def _():
        o_ref[...] = jnp.zeros_like(o_ref)

    wg = wg_ref[...].astype(BF16)
    wu = wu_ref[...].astype(BF16)
    wd = wd_ref[...].astype(BF16)
    for dst, w in zip(emit_out, (wg, wu, wd)):
        dst[...] = w
    xn = xn_ref[...]
    g = jnp.dot(xn, wg, preferred_element_type=F32)
    u = jnp.dot(xn, wu, preferred_element_type=F32)
    h = (g * jax.nn.sigmoid(g) * u).astype(BF16)
    o_ref[...] += jnp.dot(h, wd, preferred_element_type=F32)

    if n_side:
        @pl.when(pl.program_id(0) * pl.num_programs(1) + j < side_steps)
        def _():
            for src, dst in zip(side_in, side_out):
                dst[...] = src[...].astype(BF16)


def _ffn(xn, wg, wu, wd, side=(), *, tile0=0, n_tiles=None, emit_bf16=False, xn_buffers=1, out_buffers=1,
         tm=768, tf=256):
    d = xn.shape[1]
    f = wg.shape[1]
    nf = f // tf
    n_tiles = xn.shape[0] // tm - tile0 if n_tiles is None else n_tiles
    rows = n_tiles * tm
    n_steps = n_tiles * nf
    side_args, side_in_specs, side_out_specs, side_out_shape = [], [], [], []
    side_steps = max([nblk for _, _, nblk, _ in side], default=0)
    for w, rb, nblk, cols in side:
        assert rb * nblk <= w.shape[0] and nblk <= n_steps and rb % 16 == 0 and cols % 128 == 0
        spec = pl.BlockSpec((rb, cols), lambda i, j, nblk=nblk: (jnp.minimum(i * nf + j, nblk - 1), 0))
        side_args.append(w)
        side_in_specs.append(spec)
        side_out_specs.append(spec)
        side_out_shape.append(jax.ShapeDtypeStruct((rb * nblk, cols), BF16))
    w_specs = [pl.BlockSpec((d, tf), lambda i, j: (0, j)),
               pl.BlockSpec((d, tf), lambda i, j: (0, j)),
               pl.BlockSpec((tf, d), lambda i, j: (j, 0))]
    emit_specs = w_specs if emit_bf16 else []
    emit_shape = [jax.ShapeDtypeStruct(w.shape, BF16) for w in (wg, wu, wd)] if emit_bf16 else []
    outs = pl.pallas_call(
        functools.partial(_ffn_kernel, n_side=len(side), side_steps=side_steps, emit_bf16=emit_bf16),
        grid=(n_tiles, nf),
        in_specs=[pl.BlockSpec((tm, d), lambda i, j: (tile0 + i, 0), pipeline_mode=pl.Buffered(xn_buffers))]
        + w_specs + side_in_specs,
        out_specs=[pl.BlockSpec((tm, d), lambda i, j: (i, 0), pipeline_mode=pl.Buffered(out_buffers))]
        + emit_specs + side_out_specs,
        out_shape=[jax.ShapeDtypeStruct((rows, d), F32)] + emit_shape + side_out_shape,
        compiler_params=_cparams(2), name="ffn")(xn, wg, wu, wd, *side_args)
    n_emit = len(emit_shape)
    return outs[0], outs[1:1 + n_emit], outs[1 + n_emit:]


def _mm_kernel(*refs, n_pairs, trans_w):
    o_ref = refs[-1]
    acc = None
    w_dim = 1 if trans_w else 0
    for p in range(n_pairs):
        d = lax.dot_general(refs[2 * p][...], refs[2 * p + 1][...].astype(BF16),
                            (((1,), (w_dim,)), ((), ())), preferred_element_type=F32)
        acc = d if acc is None else acc + d
    o_ref[...] = acc.astype(o_ref.dtype)


def _mm(pairs, *, trans_w=False, tm=768, tn=512, out_dtype=F32):
    rows = pairs[0][0].shape[0]
    n = pairs[0][1].shape[0 if trans_w else 1]
    tn = min(tn, n)
    args, in_specs = [], []
    for a, w, kb in pairs:
        k = a.shape[1]
        args += [a, w]
        in_specs.append(pl.BlockSpec((tm, k), lambda i, j: (i, 0)))
        if trans_w:
            in_specs.append(pl.BlockSpec((tn, k), lambda i, j, kb=kb: (j, kb)))
        else:
            in_specs.append(pl.BlockSpec((k, tn), lambda i, j, kb=kb: (kb, j)))
    return pl.pallas_call(
        functools.partial(_mm_kernel, n_pairs=len(pairs), trans_w=trans_w),
        grid=(rows // tm, n // tn),
        in_specs=in_specs,
        out_specs=pl.BlockSpec((tm, tn), lambda i, j: (i, j)),
        out_shape=jax.ShapeDtypeStruct((rows, n), out_dtype),
        compiler_params=_cparams(2), name="mm")(*args)


def _cumsum_rows(x):
    t = x.shape[0]
    row = lax.broadcasted_iota(jnp.int32, x.shape, 0)
    k = 1
    while k < t:
        x = x + jnp.where(row >= k, pltpu.roll(x, k, axis=0), 0.0)
        k *= 2
    return x


def _mlstm_kernel(um_ref, v_ref, o_ref, gt_ref, cbuf_ref, c0_ref, n0_ref, m0_ref,
                  convw_ref, convb_ref, wq_ref, wk_ref, gbias_ref, hn_ref,
                  hm_ref, cout_ref, nout_ref, mout_ref,
                  xp_s, c_s, n_s, m_s, *, t, heads, dqk, dv, n_chunks):
    c = pl.program_id(1)

    @pl.when(c == 0)
    def _():
        xp_s[0:8, :] = cbuf_ref[0]
        c_s[...] = c0_ref[0]
        n_s[...] = n0_ref[0]
        m_s[...] = m0_ref[0]

    xp_s[8:8 + t, :] = um_ref[...]
    cv = convb_ref[...] + xp_s[8 - (CONV_W - 1):8 - (CONV_W - 1) + t, :] * convw_ref[0:1, :]
    for j in range(1, CONV_W):
        off = 8 - (CONV_W - 1) + j
        cv = cv + xp_s[off:off + t, :] * convw_ref[j:j + 1, :]
    cv = cv * jax.nn.sigmoid(cv)
    xp_s[0:8, :] = xp_s[t:t + 8, :]

    gt = gt_ref[...] + gbias_ref[...]
    lane = lax.broadcasted_iota(jnp.int32, gt.shape, 1)
    lf = jnp.minimum(gt, 0.0) - jnp.log1p(jnp.exp(-jnp.abs(gt)))
    g_all = jnp.where(lane < heads, gt, lf)
    b_all = _cumsum_rows(g_all)
    mix = jnp.where(lane < heads, g_all, b_all)
    tp = max(t, 128)
    if tp != t:
        mix_p = jnp.concatenate([mix, jnp.zeros((tp - t, mix.shape[1]), F32)], axis=0)
    else:
        mix_p = mix
    mix_t = mix_p.T[:, :t]

    rr = lax.broadcasted_iota(jnp.int32, (t, t), 0)
    cc = lax.broadcasted_iota(jnp.int32, (t, t), 1)
    causal = cc <= rr

    for h in range(heads):
        ig_c = mix[:, h:h + 1]
        b_c = mix[:, heads + h:heads + h + 1]
        ig_r = mix_t[h:h + 1, :]
        b_r = mix_t[heads + h:heads + h + 1, :]
        m0 = m_s[h:h + 1, 0:1]
        log_d = jnp.where(causal, b_c - b_r + ig_r, -jnp.inf)
        inter = b_c + m0
        m_t = jnp.maximum(inter, jnp.max(log_d, axis=1, keepdims=True))
        dmat = jnp.exp(log_d - m_t)
        cvh = cv[:, h * dv:(h + 1) * dv].astype(BF16)
        q = jnp.dot(cvh, wq_ref[h], preferred_element_type=F32)
        k = jnp.dot(cvh, wk_ref[h], preferred_element_type=F32) * (dqk ** -0.5)
        qb = q.astype(BF16)
        kb = k.astype(BF16)
        s = lax.dot_general(qb, kb, (((1,), (1,)), ((), ())), preferred_element_type=F32) * dmat
        w_inter = jnp.exp(inter - m_t)
        c0 = c_s[h]
        n0 = n_s[h:h + 1, :]
        vh = v_ref[:, h * dv:(h + 1) * dv]
        num = (w_inter * jnp.dot(qb, c0.astype(BF16), preferred_element_type=F32)
               + jnp.dot(s.astype(BF16), vh.astype(BF16), preferred_element_type=F32))
        den = w_inter * jnp.sum(q * n0, axis=1, keepdims=True) + jnp.sum(s, axis=1, keepdims=True)
        hh = num / jnp.maximum(jnp.abs(den), jnp.exp(-m_t))
        b_last = b_c[t - 1:t, :]
        log_w = b_last - b_c + ig_c
        m_new = jnp.maximum(b_last + m0, jnp.max(log_w, axis=0, keepdims=True))
        w_col = jnp.exp(log_w - m_new)
        decay = jnp.exp(b_last + m0 - m_new)
        wv = (w_col * vh).astype(BF16)
        c_s[h] = decay * c0 + lax.dot_general(kb, wv, (((0,), (0,)), ((), ())),
                                              preferred_element_type=F32)
        n_s[h:h + 1, :] = decay * n0 + jnp.sum(w_col * k, axis=0, keepdims=True)
        m_s[h:h + 1, :] = jnp.broadcast_to(m_new, (1, m_s.shape[1]))
        hn = _rms(hh, hn_ref[h:h + 1, :])
        og = jax.nn.sigmoid(o_ref[:, h * dv:(h + 1) * dv])
        hm_ref[:, h * dv:(h + 1) * dv] = (hn * og).astype(BF16)

    @pl.when(c == n_chunks - 1)
    def _():
        cout_ref[0] = c_s[...]
        nout_ref[0] = n_s[...]
        mout_ref[0] = m_s[...]


def _mlstm(p, gates, row0, n_streams, n_chunks, t, cbuf, c0, n0, m0,
           convw, convb, wq, wk, gbias, hnorm):
    heads, dv, dqk = wq.shape
    dm = heads * dv
    blk0 = row0 // t
    assert row0 % t == 0

    def rows(b, c):
        return blk0 + b * n_chunks + c

    kern = functools.partial(_mlstm_kernel, t=t, heads=heads, dqk=dqk, dv=dv, n_chunks=n_chunks)
    n_rows = n_streams * n_chunks * t
    outs = pl.pallas_call(
        kern,
        grid=(n_streams, n_chunks),
        in_specs=[pl.BlockSpec((t, dm), lambda b, c: (rows(b, c), 0)),
                  pl.BlockSpec((t, dm), lambda b, c: (rows(b, c), 1)),
                  pl.BlockSpec((t, dm), lambda b, c: (rows(b, c), 2)),
                  pl.BlockSpec((t, 128), lambda b, c: (rows(b, c), 0)),
                  pl.BlockSpec((1, 8, dm), lambda b, c: (b, 0, 0)),
                  pl.BlockSpec((1, heads, dqk, dv), lambda b, c: (b, 0, 0, 0)),
                  pl.BlockSpec((1, 8, dqk), lambda b, c: (b, 0, 0)),
                  pl.BlockSpec((1, 8, 128), lambda b, c: (b, 0, 0)),
                  pl.BlockSpec((CONV_W, dm), lambda b, c: (0, 0)),
                  pl.BlockSpec((1, dm), lambda b, c: (0, 0)),
                  pl.BlockSpec((heads, dv, dqk), lambda b, c: (0, 0, 0)),
                  pl.BlockSpec((heads, dv, dqk), lambda b, c: (0, 0, 0)),
                  pl.BlockSpec((1, 128), lambda b, c: (0, 0)),
                  pl.BlockSpec((heads, dv), lambda b, c: (0, 0))],
        out_specs=[pl.BlockSpec((t, dm), lambda b, c: (b * n_chunks + c, 0)),
                   pl.BlockSpec((1, heads, dqk, dv), lambda b, c: (b, 0, 0, 0)),
                   pl.BlockSpec((1, 8, dqk), lambda b, c: (b, 0, 0)),
                   pl.BlockSpec((1, 8, 128), lambda b, c: (b, 0, 0))],
        out_shape=[jax.ShapeDtypeStruct((n_rows, dm), BF16),
                   jax.ShapeDtypeStruct((n_streams, heads, dqk, dv), F32),
                   jax.ShapeDtypeStruct((n_streams, 8, dqk), F32),
                   jax.ShapeDtypeStruct((n_streams, 8, 128), F32)],
        scratch_shapes=[pltpu.VMEM((t + 8, dm), F32),
                        pltpu.VMEM((heads, dqk, dv), F32),
                        pltpu.VMEM((8, dqk), F32),
                        pltpu.VMEM((8, 128), F32)],
        compiler_params=_cparams(2), name="mlstm")(
            p, p, p, gates, cbuf, c0, n0, m0, convw, convb, wq, wk, gbias, hnorm)
    return outs


def _s5_kernel(u_ref, ax_re_ref, ax_im_ref, ldtx_ref, bt_re_ref, bt_im_ref, ct_re_ref, ct_im_ref,
               c2_re_ref, c2_im_ref, arow_re_ref, arow_im_ref, ldtrow_ref, d_ref, h0_re_ref, h0_im_ref,
               y_ref, hp_re_ref, hp_im_ref, hs_re_ref, hs_im_ref,
               h_re_s, h_im_s, bm_re_s, bm_im_s, cd_re_s, cd_im_s, cdat_re_s, cdat_im_s, k0_s,
               *, n_prompt, n_win):
    win, grp, st = S5_WIN, S5_GROUP, S5_STATE
    nc = u_ref.shape[1]
    nl = h_re_s.shape[1]

    dt = jnp.exp(ldtx_ref[...])
    ar = ax_re_ref[...]
    ai = ax_im_ref[...]
    mag = jnp.exp(ar * dt)
    ab_re = mag * jnp.cos(ai * dt)
    ab_im = mag * jnp.sin(ai * dt)
    den = ar * ar + ai * ai
    inv_re = ar / den
    inv_im = -ai / den
    f_re = ab_re - 1.0
    f_im = ab_im
    z_re = f_re * inv_re - f_im * inv_im
    z_im = f_re * inv_im + f_im * inv_re
    bb_re = z_re * bt_re_ref[...] - z_im * bt_im_ref[...]
    bb_im = z_re * bt_im_ref[...] + z_im * bt_re_ref[...]
    own_b = (lax.broadcasted_iota(jnp.int32, (nc, nl), 0) // grp
             == lax.broadcasted_iota(jnp.int32, (nc, nl), 1) // st)
    reps_b = nl // bb_re.shape[1]
    bd_re = jnp.where(own_b, jnp.concatenate([bb_re] * reps_b, axis=1), 0.0)
    bd_im = jnp.where(own_b, jnp.concatenate([bb_im] * reps_b, axis=1), 0.0)
    own_c =(lax.broadcasted_iota(jnp.int32, (nl, nc), 0) // st
             == lax.broadcasted_iota(jnp.int32, (nl, nc), 1) // grp)
    reps_c = nc // ct_re_ref.shape[1]
    cd_re_s[...] = jnp.where(own_c, jnp.concatenate([ct_re_ref[...]] * reps_c, axis=1), 0.0).astype(BF16)
    cd_im_s[...] = jnp.where(own_c, -jnp.concatenate([ct_im_ref[...]] * reps_c, axis=1), 0.0).astype(BF16)

    dtr = jnp.exp(ldtrow_ref[...])
    magr = jnp.exp(arow_re_ref[...] * dtr)
    a_re = magr * jnp.cos(arow_im_ref[...] * dtr)
    a_im = magr * jnp.sin(arow_im_ref[...] * dtr)
    aw_re, aw_im = a_re, a_im
    for _ in range(win.bit_length() - 1):
        aw_re, aw_im = aw_re * aw_re - aw_im * aw_im, 2.0 * aw_re * aw_im

    pw_re = jnp.ones_like(a_re)
    pw_im = jnp.zeros_like(a_im)
    for s in range(win - 1, -1, -1):
        bm_re_s[s * nc:(s + 1) * nc, :] = (bd_re * pw_re - bd_im * pw_im).astype(BF16)
        bm_im_s[s * nc:(s + 1) * nc, :] = (bd_re * pw_im + bd_im * pw_re).astype(BF16)
        pw_re, pw_im = pw_re * a_re - pw_im * a_im, pw_re * a_im + pw_im * a_re

    ct2_re = jnp.where(own_b, jnp.concatenate([c2_re_ref[...]] * reps_b, axis=1), 0.0)
    ct2_im = jnp.where(own_b, -jnp.concatenate([c2_im_ref[...]] * reps_b, axis=1), 0.0)
    cdat_re_s[...] = (ct2_re * a_re + ct2_im * a_im).astype(BF16)
    cdat_im_s[...] = (ct2_im * a_re - ct2_re * a_im).astype(BF16)
    k0_s[...] = (jnp.dot(bm_re_s[(win - 1) * nc:, :], cd_re_s[...], preferred_element_type=F32)
                 + jnp.dot(bm_im_s[(win - 1) * nc:, :], cd_im_s[...], preferred_element_type=F32)).astype(BF16)

    def slab(s):
        return u_ref[pl.ds(s, n_win, stride=win), :]

    u_all = jnp.concatenate([slab(s).astype(BF16) for s in range(win)], axis=1)
    h_re_s[...] = jnp.dot(u_all, bm_re_s[...], preferred_element_type=F32)
    h_im_s[...] = jnp.dot(u_all, bm_im_s[...], preferred_element_type=F32)

    h0r = h0_re_ref[...]
    h0i = h0_im_ref[...]
    hs_re_ref[...] = aw_re * h0r - aw_im * h0i + h_re_s[n_prompt:n_win, :]
    hs_im_ref[...] = aw_re * h0i + aw_im * h0r + h_im_s[n_prompt:n_win, :]
    h_re_s[n_prompt:n_win, :] = h0r
    h_im_s[n_prompt:n_win, :] = h0i

    def carry_step(c, carry):
        hr, hi = carry
        lr = h_re_s[pl.ds(c, 1), :]
        li = h_im_s[pl.ds(c, 1), :]
        h_re_s[pl.ds(c, 1), :] = hr
        h_im_s[pl.ds(c, 1), :] = hi
        return aw_re * hr - aw_im * hi + lr, aw_re * hi + aw_im * hr + li

    zero = jnp.zeros((1, nl), F32)
    hr, hi = lax.fori_loop(0, n_prompt, carry_step, (zero, zero))
    hp_re_ref[...] = jnp.broadcast_to(hr, hp_re_ref.shape)
    hp_im_ref[...] = jnp.broadcast_to(hi, hp_im_ref.shape)

    a2_re = a_re * a_re - a_im * a_im
    a2_im = 2.0 * a_re * a_im
    nt_dims = (((1,), (1,)), ((), ()))

    def step2(q, carry):
        xa = slab(2 * q)
        xb = slab(2 * q + 1)
        xa16 = xa.astype(BF16)
        hr = h_re_s[...]
        hi = h_im_s[...]
        y_a = (lax.dot_general(hr.astype(BF16), cdat_re_s[...], nt_dims, preferred_element_type=F32)
               + lax.dot_general(hi.astype(BF16), cdat_im_s[...], nt_dims, preferred_element_type=F32)
               + jnp.dot(xa16, k0_s[...], preferred_element_type=F32)
               + d_ref[...] * xa)
        y_ref[pl.ds(2 * q, n_win, stride=win), :] = y_a
        xab = jnp.concatenate([xa16, xb.astype(BF16)], axis=1)
        nr = a2_re * hr - a2_im * hi + jnp.dot(xab, bm_re_s[(win - 2) * nc:, :], preferred_element_type=F32)
        ni = a2_re * hi + a2_im * hr + jnp.dot(xab, bm_im_s[(win - 2) * nc:, :], preferred_element_type=F32)
        h_re_s[...] = nr
        h_im_s[...] = ni
        y_b = (jnp.dot(nr.astype(BF16), cd_re_s[...], preferred_element_type=F32)
               + jnp.dot(ni.astype(BF16), cd_im_s[...], preferred_element_type=F32)
               + d_ref[...] * xb)
        y_ref[pl.ds(2 * q + 1, n_win, stride=win), :] = y_b
        return carry

    lax.fori_loop(0, win // 2, step2, 0)


def _s5(proj, col0, a_re, a_im, log_dt, b_re, b_im, c_re, c_im, d, h0_re, h0_im, *, n_prompt_rows):
    groups, st = a_re.shape
    grp, win, nc = S5_GROUP, S5_WIN, S5_TILE
    gt = nc // grp
    nl = gt * st
    tiles = groups // gt
    rows = proj.shape[0]
    n_win = rows // win
    n_prompt = n_prompt_rows // win
    n_s = n_win - n_prompt
    assert col0 % nc == 0 and (groups * grp) % nc == 0 and win & (win - 1) == 0

    dup = lambda x: jnp.concatenate([x, x], axis=-1)
    rep = lambda x: jnp.repeat(x, grp, axis=0)
    ax_re = rep(dup(a_re))
    ax_im = rep(dup(a_im))
    ldtx = jnp.broadcast_to(rep(log_dt[:, None]), (groups * grp, 2 * st))
    bt_re = dup(jnp.swapaxes(b_re, 1, 2)).reshape(groups * grp, 2 * st)
    bt_im = dup(jnp.swapaxes(b_im, 1, 2)).reshape(groups * grp, 2 * st)
    ct_re = jnp.tile(jnp.swapaxes(c_re, 1, 2), (1, 1, 128 // grp)).reshape(groups * st, 128)
    ct_im = jnp.tile(jnp.swapaxes(c_im, 1, 2), (1, 1, 128 // grp)).reshape(groups * st, 128)
    c2_re = dup(c_re).reshape(groups * grp, 2 * st)
    c2_im = dup(c_im).reshape(groups * grp, 2 * st)
    arow_re = a_re.reshape(tiles, 1, nl)
    arow_im = a_im.reshape(tiles, 1, nl)
    ldtrow = jnp.repeat(log_dt, st).reshape(tiles, 1, nl)
    drow = d.reshape(tiles, 1, nc)

    rowblk = lambda w: pl.BlockSpec((nc, w), lambda j: (j, 0))
    vec = lambda w: pl.BlockSpec((None, 1, w), lambda j: (j, 0, 0))
    lanes = groups * st
    kern = functools.partial(_s5_kernel, n_prompt=n_prompt, n_win=n_win)
    return pl.pallas_call(
        kern, grid=(tiles,),
        in_specs=[pl.BlockSpec((rows, nc), lambda j: (0, col0 // nc + j)),
                  rowblk(2 * st), rowblk(2 * st), rowblk(2 * st), rowblk(2 * st), rowblk(2 * st),
                  pl.BlockSpec((nl, 128), lambda j: (j, 0)), pl.BlockSpec((nl, 128), lambda j: (j, 0)),
                  rowblk(2 * st), rowblk(2 * st),
                  vec(nl), vec(nl), vec(nl), vec(nc),
                  pl.BlockSpec((n_s, nl), lambda j: (0, j)), pl.BlockSpec((n_s, nl), lambda j: (0, j))],
        out_specs=[pl.BlockSpec((rows, nc), lambda j: (0, j)),
                   pl.BlockSpec((8, nl), lambda j: (0, j)), pl.BlockSpec((8, nl), lambda j: (0, j)),
                   pl.BlockSpec((n_s, nl), lambda j: (0, j)), pl.BlockSpec((n_s, nl), lambda j: (0, j))],
        out_shape=[jax.ShapeDtypeStruct((rows, groups * grp), F32),
                   jax.ShapeDtypeStruct((8, lanes), F32), jax.ShapeDtypeStruct((8, lanes), F32),
                   jax.ShapeDtypeStruct((n_s, lanes), F32), jax.ShapeDtypeStruct((n_s, lanes), F32)],
        scratch_shapes=[pltpu.VMEM((n_win, nl), F32), pltpu.VMEM((n_win, nl), F32),
                        pltpu.VMEM((win * nc, nl), BF16), pltpu.VMEM((win * nc, nl), BF16),
                        pltpu.VMEM((nl, nc), BF16), pltpu.VMEM((nl, nc), BF16),
                        pltpu.VMEM((nc, nl), BF16), pltpu.VMEM((nc, nl), BF16), pltpu.VMEM((nc, nc), BF16)],
        compiler_params=_cparams(1), name="s5")(
            proj, ax_re, ax_im, ldtx, bt_re, bt_im, ct_re, ct_im, c2_re, c2_im,
            arow_re, arow_im, ldtrow, drow, h0_re, h0_im)


def _glu_kernel(y_ref, w_ref, b_ref, o_ref):
    g = jax.nn.gelu(y_ref[...])
    z = jnp.dot(g.astype(BF16), w_ref[...], preferred_element_type=F32) + b_ref[...]
    o_ref[...] = (g * jax.nn.sigmoid(z)).astype(BF16)


def _glu(y, w, b, *, tm=768):
    rows, d = y.shape
    return pl.pallas_call(
        _glu_kernel, grid=(rows // tm,),
        in_specs=[pl.BlockSpec((tm, d), lambda i: (i, 0)),
                  pl.BlockSpec((d, d), lambda i: (0, 0)),
                  pl.BlockSpec((1, d), lambda i: (0, 0))],
        out_specs=pl.BlockSpec((tm, d), lambda i: (i, 0)),
        out_shape=jax.ShapeDtypeStruct((rows, d), BF16),
        compiler_params=_cparams(1), name="glu")(y, w, b.reshape(1, d))


def _pad_rows8(x, at_end=False):
    b, r, n = x.shape
    z = jnp.zeros((b, 8 - r, n), x.dtype)
    return jnp.concatenate([x, z] if not at_end else [z, x], axis=1)


def kernel(x_prompt, x_sample, state_mlstm_C, state_mlstm_n, state_mlstm_m, cache_mlstm_conv,
           state_s5_re, state_s5_im, ff1_norm_pre, ff1_norm_post, ff1_w_gate, ff1_w_up, ff1_w_down,
           mix_norm_pre, w_in, mlstm_conv_w, mlstm_conv_b, mlstm_w_q, mlstm_w_k, mlstm_b_i, mlstm_b_f,
           mlstm_head_norm, s5_a_re, s5_a_im, s5_log_dt, s5_b_re, s5_b_im, s5_c_re, s5_c_im, s5_d,
           s5_w_glu, s5_b_glu, w_out, mix_norm_post, ff2_norm_pre, ff2_norm_post, ff2_w_gate, ff2_w_up,
           ff2_w_down):
    depth = w_in.shape[0]
    bp, lp, d = x_prompt.shape
    bs, ls, _ = x_sample.shape
    assert bp == 1 and ls == S5_WIN and lp % 256 == 0
    heads, dv, dqk = mlstm_w_q.shape[1:]
    dm = heads * dv
    groups, st = s5_a_re.shape[1:]
    n_p = bp * lp
    n_s = bs * ls
    t_prompt = 256

    x = (x_prompt.reshape(n_p, d), x_sample.reshape(n_s, d))
    new_p, new_s = [], []
    for l in range(depth):
        bf = lambda w: w[l].astype(BF16)
        last = l == depth - 1
        n_main = 3 * dm + groups * S5_GROUP
        xn = _resnorm(x, g_pre=ff1_norm_pre[l], want_x=False, n_prompt=n_p)[0]
        w_in_t = jnp.swapaxes(w_in[l], 0, 1)
        d_ff = ff2_w_gate.shape[2]
        later = [(ff2_w_gate[l], 16, d // 16, d_ff), (ff2_w_up[l], 16, d // 16, d_ff),
                 (ff2_w_down[l], 64, d_ff // 64, d), (w_in_t, 32, n_main // 32, d),
                 (w_out[l], 16, d // 16, d), (s5_w_glu[l], 16, s5_w_glu.shape[1] // 16, s5_w_glu.shape[2])]
        y_a, (wg1, wu1, wd1), _ = _ffn(xn, ff1_w_gate[l], ff1_w_up[l], ff1_w_down[l],
                                       n_tiles=1, emit_bf16=True)
        y_b, _, (wg2, wu2, wd2, w_in_b, w_out_b, w_glu_b) = _ffn(xn, wg1, wu1, wd1, side=later, tile0=1,
                                                                 xn_buffers=2)
        x, hmix = _resnorm(x, (y_a, y_b), ff1_norm_post[l], mix_norm_pre[l], alpha=0.5, n_prompt=n_p)

        proj = _mm([(hmix, w_in_b, 0)], trans_w=True)
        w_gate_rows = jnp.pad(w_in_t[n_main:], ((0, 128 - 2 * heads), (0, 0)))
        gates = _mm([(hmix, w_gate_rows, 0)], trans_w=True)
        gbias = jnp.pad(jnp.concatenate([mlstm_b_i[l], mlstm_b_f[l]]), (0, 128 - 2 * heads)).reshape(1, 128)

        wq = bf(mlstm_w_q)
        wk = bf(mlstm_w_k)
        convw = mlstm_conv_w[l]
        convb = mlstm_conv_b[l].reshape(1, dm)
        hnorm = mlstm_head_norm[l]
        zc = jnp.zeros((bp, heads, dqk, dv), F32)
        hm_p, c_p, nn_p, m_p = _mlstm(
            proj, gates, 0, bp, lp // t_prompt, t_prompt,
            jnp.zeros((bp, 8, dm), F32), zc, jnp.zeros((bp, 8, dqk), F32), jnp.zeros((bp, 8, 128), F32),
            convw, convb, wq, wk, gbias, hnorm)
        m0_s = jnp.broadcast_to(_pad_rows8(state_mlstm_m[l][:, :, None]), (bs, 8, 128))
        hm_s, c_s, nn_s, m_s = _mlstm(
            proj, gates, n_p, bs, 1, ls,
            _pad_rows8(cache_mlstm_conv[l], at_end=True), state_mlstm_C[l],
            _pad_rows8(state_mlstm_n[l]), m0_s, convw, convb, wq, wk, gbias, hnorm)
        hm = jnp.concatenate([hm_p, hm_s], axis=0)
        keep = CONV_W - 1
        conv_p = proj[n_p - keep:n_p, :dm].reshape(bp, keep, dm)
        conv_s = proj[n_p:].reshape(bs, ls, -1)[:, ls - keep:, :dm]

        ys, hp_re, hp_im, hs_re, hs_im = _s5(
            proj, 3 * dm, s5_a_re[l], s5_a_im[l], s5_log_dt[l], s5_b_re[l], s5_b_im[l],
            s5_c_re[l], s5_c_im[l], s5_d[l],
            state_s5_re[l].reshape(bs, groups * st), state_s5_im[l].reshape(bs, groups * st),
            n_prompt_rows=n_p)
        ys = _glu(ys, w_glu_b, s5_b_glu[l])

        assert dm == ys.shape[1]
        mix = _mm([(hm, w_out_b, 0), (ys, w_out_b, 1)])
        x, xn = _resnorm(x, mix, mix_norm_post[l], ff2_norm_pre[l], alpha=1.0)
        y = _ffn(xn, wg2, wu2, wd2, xn_buffers=2, out_buffers=2)[0]
        x = _resnorm(x, y, ff2_norm_post[l], alpha=0.5, two_out=last, n_prompt=n_p)
        x = tuple(x) if last else (x[0][:n_p], x[0][n_p:])

        new_p.append((c_p, nn_p[:, :heads], m_p[:, :heads, 0], conv_p,
                      hp_re[:1].reshape(bp, groups, st), hp_im[:1].reshape(bp, groups, st)))
        new_s.append((c_s, nn_s[:, :heads], m_s[:, :heads, 0], conv_s,
                      hs_re.reshape(bs, groups, st), hs_im.reshape(bs, groups, st)))

    outs_p = [jnp.stack([e[i] for e in new_p]) for i in range(6)]
    outs_s = [jnp.stack([e[i] for e in new_s]) for i in range(6)]
    y_prompt = x[0].reshape(bp, lp, d)
    y_sample = x[1].reshape(bs, ls, d)
    return (y_prompt, y_sample, *outs_p, *outs_s)
```

```python
import functools

import jax
import jax.numpy as jnp
from jax import lax
from jax.experimental import pallas as pl
from jax.experimental.pallas import tpu as pltpu

F32 = jnp.float32
BF16 = jnp.bfloat16
RMS_EPS = 1e-6
CONV_W = 4
S5_WIN = 16
S5_GROUP = 16
S5_STATE = 64

V7X_VMEM_BYTES = 64 * 1024 * 1024
LANES = 128
SUBLANES = 8
VMEM_LIMIT = V7X_VMEM_BYTES * 7 // 8

S5_TILE = LANES
ROW_TILE = 768
FF_TILE = 256
COL_TILE = 512
NORM_ROWS = 256
MLSTM_CHUNK = 512


def _cparams(n_axes):
    return pltpu.CompilerParams(dimension_semantics=("arbitrary",) * n_axes,
                                vmem_limit_bytes=VMEM_LIMIT)


def _rms(v, g):
    return v * lax.rsqrt(jnp.mean(v * v, axis=-1, keepdims=True) + RMS_EPS) * g


def _resnorm_kernel(*refs, alpha, has_y, want_x, want_n, two_in, two_out, n_ptiles, n_ytiles):
    refs = list(refs)
    i = pl.program_id(0)
    if two_in:
        xp_ref = refs.pop(0)
        xs_ref = refs.pop(0)
        x = jnp.where(i < n_ptiles, xp_ref[...], xs_ref[...])
    else:
        x = refs.pop(0)[...]
    if has_y:
        if n_ytiles:
            ya_ref = refs.pop(0)
            yb_ref = refs.pop(0)
            y = jnp.where(i < n_ytiles, ya_ref[...], yb_ref[...])
        else:
            y = refs.pop(0)[...]
        gpost_ref = refs.pop(0)
        x = x + alpha * _rms(y, gpost_ref[...])
    if want_n:
        gpre_ref = refs.pop(0)
    if want_x and two_out:
        op_ref = refs.pop(0)
        os_ref = refs.pop(0)

        @pl.when(i < n_ptiles)
        def _():
            op_ref[...] = x

        @pl.when(i >= n_ptiles)
        def _():
            os_ref[...] = x
    elif want_x:
        refs.pop(0)[...] = x
    if want_n:
        refs.pop(0)[...] = _rms(x, gpre_ref[...]).astype(BF16)


def _resnorm(x, y=None, g_post=None, g_pre=None, *, alpha=1.0, want_x=True, two_out=False, n_prompt=0,
             tr=NORM_ROWS):
    two_in = isinstance(x, tuple)
    d = x[0].shape[1] if two_in else x.shape[1]
    rows = (x[0].shape[0] + x[1].shape[0]) if two_in else x.shape[0]
    n_ptiles = n_prompt // tr
    if two_in or two_out:
        assert n_prompt % tr == 0 and rows - n_prompt == tr
    has_y = y is not None
    want_n = g_pre is not None
    row_spec = pl.BlockSpec((tr, d), lambda i: (i, 0))
    vec_spec = pl.BlockSpec((1, d), lambda i: (0, 0))
    prompt_spec = pl.BlockSpec((tr, d), lambda i: (jnp.minimum(i, n_ptiles - 1), 0))
    sample_spec = pl.BlockSpec((tr, d), lambda i: (0, 0))
    if two_in:
        args, in_specs = [x[0], x[1]], [prompt_spec, sample_spec]
    else:
        args, in_specs = [x], [row_spec]
    n_ytiles = 0
    if has_y and isinstance(y, tuple):
        n_ytiles = y[0].shape[0] // tr
        assert y[0].shape[0] % tr == 0 and y[0].shape[0] + y[1].shape[0] == rows
        args += [y[0], y[1], g_post.reshape(1, d)]
        in_specs += [pl.BlockSpec((tr, d), lambda i: (jnp.minimum(i, n_ytiles - 1), 0)),
                     pl.BlockSpec((tr, d), lambda i: (jnp.maximum(i - n_ytiles, 0), 0)), vec_spec]
    elif has_y:
        args += [y, g_post.reshape(1, d)]
        in_specs += [row_spec, vec_spec]
    if want_n:
        args.append(g_pre.reshape(1, d))
        in_specs.append(vec_spec)
    out_shape, out_specs = [], []
    if want_x and two_out:
        out_shape += [jax.ShapeDtypeStruct((n_prompt, d), F32), jax.ShapeDtypeStruct((tr, d), F32)]
        out_specs += [prompt_spec, sample_spec]
    elif want_x:
        out_shape.append(jax.ShapeDtypeStruct((rows, d), F32))
        out_specs.append(row_spec)
    if want_n:
        out_shape.append(jax.ShapeDtypeStruct((rows, d), BF16))
        out_specs.append(row_spec)
    return pl.pallas_call(
        functools.partial(_resnorm_kernel, alpha=alpha, has_y=has_y, want_x=want_x, want_n=want_n,
                          two_in=two_in, two_out=two_out, n_ptiles=n_ptiles, n_ytiles=n_ytiles),
        grid=(rows // tr,), in_specs=in_specs, out_specs=out_specs, out_shape=out_shape,
        compiler_params=_cparams(1), name="resnorm")(*args)


def _ffn_kernel(*refs, n_side, side_steps, emit_bf16):
    xn_ref, wg_ref, wu_ref, wd_ref = refs[:4]
    side_in = refs[4:4 + n_side]
    o_ref = refs[4 + n_side]
    n_emit = 3 if emit_bf16 else 0
    emit_out = refs[5 + n_side:5 + n_side + n_emit]
    side_out = refs[5 + n_side + n_emit:]
    j = pl.program_id(1)

    @pl.when(j == 0)
    def _():
        o_ref[...] = jnp.zeros_like(o_ref)

    wg = wg_ref[...].astype(BF16)
    wu = wu_ref[...].astype(BF16)
    wd = wd_ref[...].astype(BF16)
    for dst, w in zip(emit_out, (wg, wu, wd)):
        dst[...] = w
    xn = xn_ref[...]
    g = jnp.dot(xn, wg, preferred_element_type=F32)
    u = jnp.dot(xn, wu, preferred_element_type=F32)
    h = (g * jax.nn.sigmoid(g) * u).astype(BF16)
    o_ref[...] += jnp.dot(h, wd, preferred_element_type=F32)

    if n_side:
        @pl.when(pl.program_id(0) * pl.num_programs(1) + j < side_steps)
        def _():
            for src, dst in zip(side_in, side_out):
                dst[...] = src[...].astype(BF16)


def _ffn(xn, wg, wu, wd, side=(), *, tile0=0, n_tiles=None, emit_bf16=False, xn_buffers=1, out_buffers=1,
         tm=ROW_TILE, tf=FF_TILE):
    d = xn.shape[1]
    f = wg.shape[1]
    nf = f // tf
    n_tiles = xn.shape[0] // tm - tile0 if n_tiles is None else n_tiles
    rows = n_tiles * tm
    n_steps = n_tiles * nf
    side_args, side_in_specs, side_out_specs, side_out_shape = [], [], [], []
    side_steps = max([nblk for _, _, nblk, _ in side], default=0)
    for w, rb, nblk, cols in side:
        assert rb * nblk <= w.shape[0] and nblk <= n_steps and rb % (2 * SUBLANES) == 0 and cols % LANES == 0
        spec = pl.BlockSpec((rb, cols), lambda i, j, nblk=nblk: (jnp.minimum(i * nf + j, nblk - 1), 0))
        side_args.append(w)
        side_in_specs.append(spec)
        side_out_specs.append(spec)
        side_out_shape.append(jax.ShapeDtypeStruct((rb * nblk, cols), BF16))
    w_specs = [pl.BlockSpec((d, tf), lambda i, j: (0, j)),
               pl.BlockSpec((d, tf), lambda i, j: (0, j)),
               pl.BlockSpec((tf, d), lambda i, j: (j, 0))]
    emit_specs = w_specs if emit_bf16 else []
    emit_shape = [jax.ShapeDtypeStruct(w.shape, BF16) for w in (wg, wu, wd)] if emit_bf16 else []
    outs = pl.pallas_call(
        functools.partial(_ffn_kernel, n_side=len(side), side_steps=side_steps, emit_bf16=emit_bf16),
        grid=(n_tiles, nf),
        in_specs=[pl.BlockSpec((tm, d), lambda i, j: (tile0 + i, 0), pipeline_mode=pl.Buffered(xn_buffers))]
        + w_specs + side_in_specs,
        out_specs=[pl.BlockSpec((tm, d), lambda i, j: (i, 0), pipeline_mode=pl.Buffered(out_buffers))]
        + emit_specs + side_out_specs,
        out_shape=[jax.ShapeDtypeStruct((rows, d), F32)] + emit_shape + side_out_shape,
        compiler_params=_cparams(2), name="ffn")(xn, wg, wu, wd, *side_args)
    n_emit = len(emit_shape)
    return outs[0], outs[1:1 + n_emit], outs[1 + n_emit:]


def _mm_kernel(*refs, n_pairs, trans_w):
    o_ref = refs[-1]
    acc = None
    w_dim = 1 if trans_w else 0
    for p in range(n_pairs):
        d = lax.dot_general(refs[2 * p][...], refs[2 * p + 1][...].astype(BF16),
                            (((1,), (w_dim,)), ((), ())), preferred_element_type=F32)
        acc = d if acc is None else acc + d
    o_ref[...] = acc.astype(o_ref.dtype)


def _mm(pairs, *, trans_w=False, tm=ROW_TILE, tn=COL_TILE, out_dtype=F32):
    rows = pairs[0][0].shape[0]
    n = pairs[0][1].shape[0 if trans_w else 1]
    tn = min(tn, n)
    args, in_specs = [], []
    for a, w, kb in pairs:
        k = a.shape[1]
        args += [a, w]
        in_specs.append(pl.BlockSpec((tm, k), lambda i, j: (i, 0)))
        if trans_w:
            in_specs.append(pl.BlockSpec((tn, k), lambda i, j, kb=kb: (j, kb)))
        else:
            in_specs.append(pl.BlockSpec((k, tn), lambda i, j, kb=kb: (kb, j)))
    return pl.pallas_call(
        functools.partial(_mm_kernel, n_pairs=len(pairs), trans_w=trans_w),
        grid=(rows // tm, n // tn),
        in_specs=in_specs,
        out_specs=pl.BlockSpec((tm, tn), lambda i, j: (i, j)),
        out_shape=jax.ShapeDtypeStruct((rows, n), out_dtype),
        compiler_params=_cparams(2), name="mm")(*args)


def _cumsum_rows(x):
    t = x.shape[0]
    row = lax.broadcasted_iota(jnp.int32, x.shape, 0)
    k = 1
    while k < t:
        x = x + jnp.where(row >= k, pltpu.roll(x, k, axis=0), 0.0)
        k *= 2
    return x


def _mlstm_kernel(um_ref, v_ref, o_ref, gt_ref, cbuf_ref, c0_ref, n0_ref, m0_ref,
                  convw_ref, convb_ref, wq_ref, wk_ref, gbias_ref, hn_ref,
                  hm_ref, cout_ref, nout_ref, mout_ref,
                  xp_s, c_s, n_s, m_s, *, t, heads, dqk, dv, n_chunks):
    c = pl.program_id(1)

    @pl.when(c == 0)
    def _():
        xp_s[0:8, :] = cbuf_ref[0]
        c_s[...] = c0_ref[0]
        n_s[...] = n0_ref[0]
        m_s[...] = m0_ref[0]

    xp_s[8:8 + t, :] = um_ref[...]
    cv = convb_ref[...] + xp_s[8 - (CONV_W - 1):8 - (CONV_W - 1) + t, :] * convw_ref[0:1, :]
    for j in range(1, CONV_W):
        off = 8 - (CONV_W - 1) + j
        cv = cv + xp_s[off:off + t, :] * convw_ref[j:j + 1, :]
    cv = cv * jax.nn.sigmoid(cv)
    xp_s[0:8, :] = xp_s[t:t + 8, :]

    gt = gt_ref[...] + gbias_ref[...]
    lane = lax.broadcasted_iota(jnp.int32, gt.shape, 1)
    lf = jnp.minimum(gt, 0.0) - jnp.log1p(jnp.exp(-jnp.abs(gt)))
    g_all = jnp.where(lane < heads, gt, lf)
    b_all = _cumsum_rows(g_all)
    mix = jnp.where(lane < heads, g_all, b_all)
    tp = max(t, LANES)
    if tp != t:
        mix_p = jnp.concatenate([mix, jnp.zeros((tp - t, mix.shape[1]), F32)], axis=0)
    else:
        mix_p = mix
    mix_t = mix_p.T[:, :t]

    rr = lax.broadcasted_iota(jnp.int32, (t, t), 0)
    cc = lax.broadcasted_iota(jnp.int32, (t, t), 1)
    causal = cc <= rr

    for h in range(heads):
        ig_c = mix[:, h:h + 1]
        b_c = mix[:, heads + h:heads + h + 1]
        ig_r = mix_t[h:h + 1, :]
        b_r = mix_t[heads + h:heads + h + 1, :]
        m0 = m_s[h:h + 1, 0:1]
        log_d = jnp.where(causal, b_c - b_r + ig_r, -jnp.inf)
        inter = b_c + m0
        m_t = jnp.maximum(inter, jnp.max(log_d, axis=1, keepdims=True))
        dmat = jnp.exp(log_d - m_t)
        cvh = cv[:, h * dv:(h + 1) * dv].astype(BF16)
        q = jnp.dot(cvh, wq_ref[h], preferred_element_type=F32)
        k = jnp.dot(cvh, wk_ref[h], preferred_element_type=F32) * (dqk ** -0.5)
        qb = q.astype(BF16)
        kb = k.astype(BF16)
        s = lax.dot_general(qb, kb, (((1,), (1,)), ((), ())), preferred_element_type=F32) * dmat
        w_inter = jnp.exp(inter - m_t)
        c0 = c_s[h]
        n0 = n_s[h:h + 1, :]
        vh = v_ref[:, h * dv:(h + 1) * dv]
        num = (w_inter * jnp.dot(qb, c0.astype(BF16), preferred_element_type=F32)
               + jnp.dot(s.astype(BF16), vh.astype(BF16), preferred_element_type=F32))
        den = w_inter * jnp.sum(q * n0, axis=1, keepdims=True) + jnp.sum(s, axis=1, keepdims=True)
        hh = num / jnp.maximum(jnp.abs(den), jnp.exp(-m_t))
        b_last = b_c[t - 1:t, :]
        log_w = b_last - b_c + ig_c
        m_new = jnp.maximum(b_last + m0, jnp.max(log_w, axis=0, keepdims=True))
        w_col = jnp.exp(log_w - m_new)
        decay = jnp.exp(b_last + m0 - m_new)
        wv = (w_col * vh).astype(BF16)
        c_s[h] = decay * c0 + lax.dot_general(kb, wv, (((0,), (0,)), ((), ())),
                                              preferred_element_type=F32)
        n_s[h:h + 1, :] = decay * n0 + jnp.sum(w_col * k, axis=0, keepdims=True)
        m_s[h:h + 1, :] = jnp.broadcast_to(m_new, (1, m_s.shape[1]))
        hn = _rms(hh, hn_ref[h:h + 1, :])
        og = jax.nn.sigmoid(o_ref[:, h * dv:(h + 1) * dv])
        hm_ref[:, h * dv:(h + 1) * dv] = (hn * og).astype(BF16)

    @pl.when(c == n_chunks - 1)
    def _():
        cout_ref[0] = c_s[...]
        nout_ref[0] = n_s[...]
        mout_ref[0] = m_s[...]


def _mlstm(p, gates, row0, n_streams, n_chunks, t, cbuf, c0, n0, m0,
           convw, convb, wq, wk, gbias, hnorm):
    heads, dv, dqk = wq.shape
    dm = heads * dv
    blk0 = row0 // t
    assert row0 % t == 0

    def rows(b, c):
        return blk0 + b * n_chunks + c

    kern = functools.partial(_mlstm_kernel, t=t, heads=heads, dqk=dqk, dv=dv, n_chunks=n_chunks)
    n_rows = n_streams * n_chunks * t
    outs = pl.pallas_call(
        kern,
        grid=(n_streams, n_chunks),
        in_specs=[pl.BlockSpec((t, dm), lambda b, c: (rows(b, c), 0)),
                  pl.BlockSpec((t, dm), lambda b, c: (rows(b, c), 1)),
                  pl.BlockSpec((t, dm), lambda b, c: (rows(b, c), 2)),
                  pl.BlockSpec((t, LANES), lambda b, c: (rows(b, c), 0)),
                  pl.BlockSpec((1, 8, dm), lambda b, c: (b, 0, 0)),
                  pl.BlockSpec((1, heads, dqk, dv), lambda b, c: (b, 0, 0, 0)),
                  pl.BlockSpec((1, 8, dqk), lambda b, c: (b, 0, 0)),
                  pl.BlockSpec((1, SUBLANES, LANES), lambda b, c: (b, 0, 0)),
                  pl.BlockSpec((CONV_W, dm), lambda b, c: (0, 0)),
                  pl.BlockSpec((1, dm), lambda b, c: (0, 0)),
                  pl.BlockSpec((heads, dv, dqk), lambda b, c: (0, 0, 0)),
                  pl.BlockSpec((heads, dv, dqk), lambda b, c: (0, 0, 0)),
                  pl.BlockSpec((1, LANES), lambda b, c: (0, 0)),
                  pl.BlockSpec((heads, dv), lambda b, c: (0, 0))],
        out_specs=[pl.BlockSpec((t, dm), lambda b, c: (b * n_chunks + c, 0)),
                   pl.BlockSpec((1, heads, dqk, dv), lambda b, c: (b, 0, 0, 0)),
                   pl.BlockSpec((1, 8, dqk), lambda b, c: (b, 0, 0)),
                   pl.BlockSpec((1, SUBLANES, LANES), lambda b, c: (b, 0, 0))],
        out_shape=[jax.ShapeDtypeStruct((n_rows, dm), BF16),
                   jax.ShapeDtypeStruct((n_streams, heads, dqk, dv), F32),
                   jax.ShapeDtypeStruct((n_streams, 8, dqk), F32),
                   jax.ShapeDtypeStruct((n_streams, SUBLANES, LANES), F32)],
        scratch_shapes=[pltpu.VMEM((t + 8, dm), F32),
                        pltpu.VMEM((heads, dqk, dv), F32),
                        pltpu.VMEM((8, dqk), F32),
                        pltpu.VMEM((SUBLANES, LANES), F32)],
        compiler_params=_cparams(2), name="mlstm")(
            p, p, p, gates, cbuf, c0, n0, m0, convw, convb, wq, wk, gbias, hnorm)
    return outs


def _s5_kernel(u_ref, ax_re_ref, ax_im_ref, ldtx_ref, bt_re_ref, bt_im_ref, ct_re_ref, ct_im_ref,
               c2_re_ref, c2_im_ref, arow_re_ref, arow_im_ref, ldtrow_ref, d_ref, h0_re_ref, h0_im_ref,
               y_ref, hp_re_ref, hp_im_ref, hs_re_ref, hs_im_ref,
               h_re_s, h_im_s, bm_re_s, bm_im_s, cd_re_s, cd_im_s, cdat_re_s, cdat_im_s, k0_s,
               *, n_prompt, n_win):
    win, grp, st = S5_WIN, S5_GROUP, S5_STATE
    nc = u_ref.shape[1]
    nl = h_re_s.shape[1]

    dt = jnp.exp(ldtx_ref[...])
    ar = ax_re_ref[...]
    ai = ax_im_ref[...]
    mag = jnp.exp(ar * dt)
    ab_re = mag * jnp.cos(ai * dt)
    ab_im = mag * jnp.sin(ai * dt)
    den = ar * ar + ai * ai
    inv_re = ar / den
    inv_im = -ai / den
    f_re = ab_re - 1.0
    f_im = ab_im
    z_re = f_re * inv_re - f_im * inv_im
    z_im = f_re * inv_im + f_im * inv_re
    bb_re = z_re * bt_re_ref[...] - z_im * bt_im_ref[...]
    bb_im = z_re * bt_im_ref[...] + z_im * bt_re_ref[...]
    own_b = (lax.broadcasted_iota(jnp.int32, (nc, nl), 0) // grp
             == lax.broadcasted_iota(jnp.int32, (nc, nl), 1) // st)
    reps_b = nl // bb_re.shape[1]
    bd_re = jnp.where(own_b, jnp.concatenate([bb_re] * reps_b, axis=1), 0.0)
    bd_im = jnp.where(own_b, jnp.concatenate([bb_im] * reps_b, axis=1), 0.0)
    own_c =(lax.broadcasted_iota(jnp.int32, (nl, nc), 0) // st
             == lax.broadcasted_iota(jnp.int32, (nl, nc), 1) // grp)
    reps_c = nc // ct_re_ref.shape[1]
    cd_re_s[...] = jnp.where(own_c, jnp.concatenate([ct_re_ref[...]] * reps_c, axis=1), 0.0).astype(BF16)
    cd_im_s[...] = jnp.where(own_c, -jnp.concatenate([ct_im_ref[...]] * reps_c, axis=1), 0.0).astype(BF16)

    dtr = jnp.exp(ldtrow_ref[...])
    magr = jnp.exp(arow_re_ref[...] * dtr)
    a_re = magr * jnp.cos(arow_im_ref[...] * dtr)
    a_im = magr * jnp.sin(arow_im_ref[...] * dtr)
    aw_re, aw_im = a_re, a_im
    for _ in range(win.bit_length() - 1):
        aw_re, aw_im = aw_re * aw_re - aw_im * aw_im, 2.0 * aw_re * aw_im

    pw_re = jnp.ones_like(a_re)
    pw_im = jnp.zeros_like(a_im)
    for s in range(win - 1, -1, -1):
        bm_re_s[s * nc:(s + 1) * nc, :] = (bd_re * pw_re - bd_im * pw_im).astype(BF16)
        bm_im_s[s * nc:(s + 1) * nc, :] = (bd_re * pw_im + bd_im * pw_re).astype(BF16)
        pw_re, pw_im = pw_re * a_re - pw_im * a_im, pw_re * a_im + pw_im * a_re

    ct2_re = jnp.where(own_b, jnp.concatenate([c2_re_ref[...]] * reps_b, axis=1), 0.0)
    ct2_im = jnp.where(own_b, -jnp.concatenate([c2_im_ref[...]] * reps_b, axis=1), 0.0)
    cdat_re_s[...] = (ct2_re * a_re + ct2_im * a_im).astype(BF16)
    cdat_im_s[...] = (ct2_im * a_re - ct2_re * a_im).astype(BF16)
    k0_s[...] = (jnp.dot(bm_re_s[(win - 1) * nc:, :], cd_re_s[...], preferred_element_type=F32)
                 + jnp.dot(bm_im_s[(win - 1) * nc:, :], cd_im_s[...], preferred_element_type=F32)).astype(BF16)

    def slab(s):
        return u_ref[pl.ds(s, n_win, stride=win), :]

    u_all = jnp.concatenate([slab(s).astype(BF16) for s in range(win)], axis=1)
    h_re_s[...] = jnp.dot(u_all, bm_re_s[...], preferred_element_type=F32)
    h_im_s[...] = jnp.dot(u_all, bm_im_s[...], preferred_element_type=F32)

    h0r = h0_re_ref[...]
    h0i = h0_im_ref[...]
    hs_re_ref[...] = aw_re * h0r - aw_im * h0i + h_re_s[n_prompt:n_win, :]
    hs_im_ref[...] = aw_re * h0i + aw_im * h0r + h_im_s[n_prompt:n_win, :]
    h_re_s[n_prompt:n_win, :] = h0r
    h_im_s[n_prompt:n_win, :] = h0i

    def carry_step(c, carry):
        hr, hi = carry
        lr = h_re_s[pl.ds(c, 1), :]
        li = h_im_s[pl.ds(c, 1), :]
        h_re_s[pl.ds(c, 1), :] = hr
        h_im_s[pl.ds(c, 1), :] = hi
        return aw_re * hr - aw_im * hi + lr, aw_re * hi + aw_im * hr + li

    zero = jnp.zeros((1, nl), F32)
    hr, hi = lax.fori_loop(0, n_prompt, carry_step, (zero, zero))
    hp_re_ref[...] = jnp.broadcast_to(hr, hp_re_ref.shape)
    hp_im_ref[...] = jnp.broadcast_to(hi, hp_im_ref.shape)

    a2_re = a_re * a_re - a_im * a_im
    a2_im = 2.0 * a_re * a_im
    nt_dims = (((1,), (1,)), ((), ()))

    def step2(q, carry):
        xa = slab(2 * q)
        xb = slab(2 * q + 1)
        xa16 = xa.astype(BF16)
        hr = h_re_s[...]
        hi = h_im_s[...]
        y_a = (lax.dot_general(hr.astype(BF16), cdat_re_s[...], nt_dims, preferred_element_type=F32)
               + lax.dot_general(hi.astype(BF16), cdat_im_s[...], nt_dims, preferred_element_type=F32)
               + jnp.dot(xa16, k0_s[...], preferred_element_type=F32)
               + d_ref[...] * xa)
        y_ref[pl.ds(2 * q, n_win, stride=win), :] = y_a
        xab = jnp.concatenate([xa16, xb.astype(BF16)], axis=1)
        nr = a2_re * hr - a2_im * hi + jnp.dot(xab, bm_re_s[(win - 2) * nc:, :], preferred_element_type=F32)
        ni = a2_re * hi + a2_im * hr + jnp.dot(xab, bm_im_s[(win - 2) * nc:, :], preferred_element_type=F32)
        h_re_s[...] = nr
        h_im_s[...] = ni
        y_b = (jnp.dot(nr.astype(BF16), cd_re_s[...], preferred_element_type=F32)
               + jnp.dot(ni.astype(BF16), cd_im_s[...], preferred_element_type=F32)
               + d_ref[...] * xb)
        y_ref[pl.ds(2 * q + 1, n_win, stride=win), :] = y_b
        return carry

    lax.fori_loop(0, win // 2, step2, 0, unroll=2)


def _s5(proj, col0, a_re, a_im, log_dt, b_re, b_im, c_re, c_im, d, h0_re, h0_im, *, n_prompt_rows):
    groups, st = a_re.shape
    grp, win, nc = S5_GROUP, S5_WIN, S5_TILE
    gt = nc // grp
    nl = gt * st
    tiles = groups // gt
    rows = proj.shape[0]
    n_win = rows // win
    n_prompt = n_prompt_rows // win
    n_s = n_win - n_prompt
    assert col0 % nc == 0 and (groups * grp) % nc == 0 and win & (win - 1) == 0

    dup = lambda x: jnp.concatenate([x, x], axis=-1)
    rep = lambda x: jnp.repeat(x, grp, axis=0)
    ax_re = rep(dup(a_re))
    ax_im = rep(dup(a_im))
    ldtx = jnp.broadcast_to(rep(log_dt[:, None]), (groups * grp, 2 * st))
    bt_re = dup(jnp.swapaxes(b_re, 1, 2)).reshape(groups * grp, 2 * st)
    bt_im = dup(jnp.swapaxes(b_im, 1, 2)).reshape(groups * grp, 2 * st)
    ct_re = jnp.tile(jnp.swapaxes(c_re, 1, 2), (1, 1, LANES // grp)).reshape(groups * st, LANES)
    ct_im = jnp.tile(jnp.swapaxes(c_im, 1, 2), (1, 1, LANES // grp)).reshape(groups * st, LANES)
    c2_re = dup(c_re).reshape(groups * grp, 2 * st)
    c2_im = dup(c_im).reshape(groups * grp, 2 * st)
    arow_re = a_re.reshape(tiles, 1, nl)
    arow_im = a_im.reshape(tiles, 1, nl)
    ldtrow = jnp.repeat(log_dt, st).reshape(tiles, 1, nl)
    drow = d.reshape(tiles, 1, nc)

    rowblk = lambda w: pl.BlockSpec((nc, w), lambda j: (j, 0))
    vec = lambda w: pl.BlockSpec((None, 1, w), lambda j: (j, 0, 0))
    lanes = groups * st
    kern = functools.partial(_s5_kernel, n_prompt=n_prompt, n_win=n_win)
    return pl.pallas_call(
        kern, grid=(tiles,),
        in_specs=[pl.BlockSpec((rows, nc), lambda j: (0, col0 // nc + j)),
                  rowblk(2 * st), rowblk(2 * st), rowblk(2 * st), rowblk(2 * st), rowblk(2 * st),
                  pl.BlockSpec((nl, LANES), lambda j: (j, 0)), pl.BlockSpec((nl, LANES), lambda j: (j, 0)),
                  rowblk(2 * st), rowblk(2 * st),
                  vec(nl), vec(nl), vec(nl), vec(nc),
                  pl.BlockSpec((n_s, nl), lambda j: (0, j)), pl.BlockSpec((n_s, nl), lambda j: (0, j))],
        out_specs=[pl.BlockSpec((rows, nc), lambda j: (0, j)),
                   pl.BlockSpec((8, nl), lambda j: (0, j)), pl.BlockSpec((8, nl), lambda j: (0, j)),
                   pl.BlockSpec((n_s, nl), lambda j: (0, j)), pl.BlockSpec((n_s, nl), lambda j: (0, j))],
        out_shape=[jax.ShapeDtypeStruct((rows, groups * grp), F32),
                   jax.ShapeDtypeStruct((8, lanes), F32), jax.ShapeDtypeStruct((8, lanes), F32),
                   jax.ShapeDtypeStruct((n_s, lanes), F32), jax.ShapeDtypeStruct((n_s, lanes), F32)],
        scratch_shapes=[pltpu.VMEM((n_win, nl), F32), pltpu.VMEM((n_win, nl), F32),
                        pltpu.VMEM((win * nc, nl), BF16), pltpu.VMEM((win * nc, nl), BF16),
                        pltpu.VMEM((nl, nc), BF16), pltpu.VMEM((nl, nc), BF16),
                        pltpu.VMEM((nc, nl), BF16), pltpu.VMEM((nc, nl), BF16), pltpu.VMEM((nc, nc), BF16)],
        compiler_params=_cparams(1), name="s5")(
            proj, ax_re, ax_im, ldtx, bt_re, bt_im, ct_re, ct_im, c2_re, c2_im,
            arow_re, arow_im, ldtrow, drow, h0_re, h0_im)


def _glu_kernel(y_ref, w_ref, b_ref, o_ref):
    g = jax.nn.gelu(y_ref[...])
    z = jnp.dot(g.astype(BF16), w_ref[...], preferred_element_type=F32) + b_ref[...]
    o_ref[...] = (g * jax.nn.sigmoid(z)).astype(BF16)


def _glu(y, w, b, *, tm=ROW_TILE):
    rows, d = y.shape
    return pl.pallas_call(
        _glu_kernel, grid=(rows // tm,),
        in_specs=[pl.BlockSpec((tm, d), lambda i: (i, 0)),
                  pl.BlockSpec((d, d), lambda i: (0, 0)),
                  pl.BlockSpec((1, d), lambda i: (0, 0))],
        out_specs=pl.BlockSpec((tm, d), lambda i: (i, 0)),
        out_shape=jax.ShapeDtypeStruct((rows, d), BF16),
        compiler_params=_cparams(1), name="glu")(y, w, b.reshape(1, d))


def _pad_rows8(x, at_end=False):
    b, r, n = x.shape
    z = jnp.zeros((b, 8 - r, n), x.dtype)
    return jnp.concatenate([x, z] if not at_end else [z, x], axis=1)


def kernel(x_prompt, x_sample, state_mlstm_C, state_mlstm_n, state_mlstm_m, cache_mlstm_conv,
           state_s5_re, state_s5_im, ff1_norm_pre, ff1_norm_post, ff1_w_gate, ff1_w_up, ff1_w_down,
           mix_norm_pre, w_in, mlstm_conv_w, mlstm_conv_b, mlstm_w_q, mlstm_w_k, mlstm_b_i, mlstm_b_f,
           mlstm_head_norm, s5_a_re, s5_a_im, s5_log_dt, s5_b_re, s5_b_im, s5_c_re, s5_c_im, s5_d,
           s5_w_glu, s5_b_glu, w_out, mix_norm_post, ff2_norm_pre, ff2_norm_post, ff2_w_gate, ff2_w_up,
           ff2_w_down):
    depth = w_in.shape[0]
    bp, lp, d = x_prompt.shape
    bs, ls, _ = x_sample.shape
    assert bp == 1 and ls == S5_WIN and lp % MLSTM_CHUNK == 0
    heads, dv, dqk = mlstm_w_q.shape[1:]
    dm = heads * dv
    groups, st = s5_a_re.shape[1:]
    n_p = bp * lp
    n_s = bs * ls
    t_prompt = MLSTM_CHUNK

    x = (x_prompt.reshape(n_p, d), x_sample.reshape(n_s, d))
    new_p, new_s = [], []
    for l in range(depth):
        bf = lambda w: w[l].astype(BF16)
        last = l == depth - 1
        n_main = 3 * dm + groups * S5_GROUP
        xn = _resnorm(x, g_pre=ff1_norm_pre[l], want_x=False, n_prompt=n_p)[0]
        w_in_t = jnp.swapaxes(w_in[l], 0, 1)
        d_ff = ff2_w_gate.shape[2]
        later = [(ff2_w_gate[l], 16, d // 16, d_ff), (ff2_w_up[l], 16, d // 16, d_ff),
                 (ff2_w_down[l], 64, d_ff // 64, d), (w_in_t, 32, n_main // 32, d),
                 (w_out[l], 16, d // 16, d), (s5_w_glu[l], 16, s5_w_glu.shape[1] // 16, s5_w_glu.shape[2])]
        y_a, (wg1, wu1, wd1), _ = _ffn(xn, ff1_w_gate[l], ff1_w_up[l], ff1_w_down[l],
                                       n_tiles=1, emit_bf16=True)
        y_b, _, (wg2, wu2, wd2, w_in_b, w_out_b, w_glu_b) = _ffn(xn, wg1, wu1, wd1, side=later, tile0=1,
                                                                 xn_buffers=2)
        x, hmix = _resnorm(x, (y_a, y_b), ff1_norm_post[l], mix_norm_pre[l], alpha=0.5, n_prompt=n_p)

        proj = _mm([(hmix, w_in_b, 0)], trans_w=True)
        w_gate_rows = jnp.pad(w_in_t[n_main:], ((0, 128 - 2 * heads), (0, 0)))
        gates = _mm([(hmix, w_gate_rows, 0)], trans_w=True)
        gbias = jnp.pad(jnp.concatenate([mlstm_b_i[l], mlstm_b_f[l]]), (0, LANES - 2 * heads)).reshape(1, LANES)

        wq = bf(mlstm_w_q)
        wk = bf(mlstm_w_k)
        convw = mlstm_conv_w[l]
        convb = mlstm_conv_b[l].reshape(1, dm)
        hnorm = mlstm_head_norm[l]
        zc = jnp.zeros((bp, heads, dqk, dv), F32)
        hm_p, c_p, nn_p, m_p = _mlstm(
            proj, gates, 0, bp, lp // t_prompt, t_prompt,
            jnp.zeros((bp, SUBLANES, dm), F32), zc, jnp.zeros((bp, SUBLANES, dqk), F32),
            jnp.zeros((bp, SUBLANES, LANES), F32),
            convw, convb, wq, wk, gbias, hnorm)
        m0_s = jnp.broadcast_to(_pad_rows8(state_mlstm_m[l][:, :, None]), (bs, SUBLANES, LANES))
        hm_s, c_s, nn_s, m_s = _mlstm(
            proj, gates, n_p, bs, 1, ls,
            _pad_rows8(cache_mlstm_conv[l], at_end=True), state_mlstm_C[l],
            _pad_rows8(state_mlstm_n[l]), m0_s, convw, convb, wq, wk, gbias, hnorm)
        hm = jnp.concatenate([hm_p, hm_s], axis=0)
        keep = CONV_W - 1
        conv_p = proj[n_p - keep:n_p, :dm].reshape(bp, keep, dm)
        conv_s = proj[n_p:].reshape(bs, ls, -1)[:, ls - keep:, :dm]

        ys, hp_re, hp_im, hs_re, hs_im = _s5(
            proj, 3 * dm, s5_a_re[l], s5_a_im[l], s5_log_dt[l], s5_b_re[l], s5_b_im[l],
            s5_c_re[l], s5_c_im[l], s5_d[l],
            state_s5_re[l].reshape(bs, groups * st), state_s5_im[l].reshape(bs, groups * st),
            n_prompt_rows=n_p)
        ys = _glu(ys, w_glu_b, s5_b_glu[l])

        assert dm == ys.shape[1]
        mix = _mm([(hm, w_out_b, 0), (ys, w_out_b, 1)])
        x, xn = _resnorm(x, mix, mix_norm_post[l], ff2_norm_pre[l], alpha=1.0)
        y = _ffn(xn, wg2, wu2, wd2, xn_buffers=2, out_buffers=2)[0]
        x = _resnorm(x, y, ff2_norm_post[l], alpha=0.5, two_out=last, n_prompt=n_p)
        x = tuple(x) if last else (x[0][:n_p], x[0][n_p:])

        new_p.append((c_p, nn_p[:, :heads], m_p[:, :heads, 0], conv_p,
                      hp_re[:1].reshape(bp, groups, st), hp_im[:1].reshape(bp, groups, st)))
        new_s.append((c_s, nn_s[:, :heads], m_s[:, :heads, 0], conv_s,
                      hs_re.reshape(bs, groups, st), hs_im.reshape(bs, groups, st)))

    outs_p = [jnp.stack([e[i] for e in new_p]) for i in range(6)]
    outs_s = [jnp.stack([e[i] for e in new_s]) for i in range(6)]
    y_prompt = x[0].reshape(bp, lp, d)
    y_sample = x[1].reshape(bs, ls, d)
    return (y_prompt, y_sample, *outs_p, *outs_s)
```

```python
import functools

import jax
import jax.numpy as jnp
from jax import lax
from jax.experimental import pallas as pl
from jax.experimental.pallas import tpu as pltpu

F32 = jnp.float32
BF16 = jnp.bfloat16
RMS_EPS = 1e-6
CONV_W = 4
S5_WIN = 16
S5_GROUP = 16
S5_STATE = 64

V7X_VMEM_BYTES = 64 * 1024 * 1024
LANES = 128
SUBLANES = 8
VMEM_LIMIT = V7X_VMEM_BYTES * 7 // 8

S5_TILE = LANES
ROW_TILE = 768
MM_ROW_TILE = 1056
FF_TILE = 256
COL_TILE = 1024
NORM_ROWS = 256
MLSTM_CHUNK = 512


def _cparams(n_axes):
    return pltpu.CompilerParams(dimension_semantics=("arbitrary",) * n_axes,
                                vmem_limit_bytes=VMEM_LIMIT)


def _rms(v, g):
    return v * lax.rsqrt(jnp.mean(v * v, axis=-1, keepdims=True) + RMS_EPS) * g


def _resnorm_kernel(*refs, alpha, has_y, want_x, want_n, two_in, two_out, n_ptiles, n_ytiles):
    refs = list(refs)
    i = pl.program_id(0)
    if two_in:
        xp_ref = refs.pop(0)
        xs_ref = refs.pop(0)
        x = jnp.where(i < n_ptiles, xp_ref[...], xs_ref[...])
    else:
        x = refs.pop(0)[...]
    if has_y:
        if n_ytiles:
            ya_ref = refs.pop(0)
            yb_ref = refs.pop(0)
            y = jnp.where(i < n_ytiles, ya_ref[...], yb_ref[...])
        else:
            y = refs.pop(0)[...]
        gpost_ref = refs.pop(0)
        x = x + alpha * _rms(y, gpost_ref[...])
    if want_n:
        gpre_ref = refs.pop(0)
    if want_x and two_out:
        op_ref = refs.pop(0)
        os_ref = refs.pop(0)

        @pl.when(i < n_ptiles)
        def _():
            op_ref[...] = x

        @pl.when(i >= n_ptiles)
        def _():
            os_ref[...] = x
    elif want_x:
        refs.pop(0)[...] = x
    if want_n:
        refs.pop(0)[...] = _rms(x, gpre_ref[...]).astype(BF16)


def _resnorm(x, y=None, g_post=None, g_pre=None, *, alpha=1.0, want_x=True, two_out=False, n_prompt=0,
             tr=NORM_ROWS):
    two_in = isinstance(x, tuple)
    d = x[0].shape[1] if two_in else x.shape[1]
    rows = (x[0].shape[0] + x[1].shape[0]) if two_in else x.shape[0]
    n_ptiles = n_prompt // tr
    if two_in or two_out:
        assert n_prompt % tr == 0 and rows - n_prompt == tr
    has_y = y is not None
    want_n = g_pre is not None
    row_spec = pl.BlockSpec((tr, d), lambda i: (i, 0))
    vec_spec = pl.BlockSpec((1, d), lambda i: (0, 0))
    prompt_spec = pl.BlockSpec((tr, d), lambda i: (jnp.minimum(i, n_ptiles - 1), 0))
    sample_spec = pl.BlockSpec((tr, d), lambda i: (0, 0))
    if two_in:
        args, in_specs = [x[0], x[1]], [prompt_spec, sample_spec]
    else:
        args, in_specs = [x], [row_spec]
    n_ytiles = 0
    if has_y and isinstance(y, tuple):
        n_ytiles = y[0].shape[0] // tr
        assert y[0].shape[0] % tr == 0 and y[0].shape[0] + y[1].shape[0] == rows
        args += [y[0], y[1], g_post.reshape(1, d)]
        in_specs += [pl.BlockSpec((tr, d), lambda i: (jnp.minimum(i, n_ytiles - 1), 0)),
                     pl.BlockSpec((tr, d), lambda i: (jnp.maximum(i - n_ytiles, 0), 0)), vec_spec]
    elif has_y:
        args += [y, g_post.reshape(1, d)]
        in_specs += [row_spec, vec_spec]
    if want_n:
        args.append(g_pre.reshape(1, d))
        in_specs.append(vec_spec)
    out_shape, out_specs = [], []
    if want_x and two_out:
        out_shape += [jax.ShapeDtypeStruct((n_prompt, d), F32), jax.ShapeDtypeStruct((tr, d), F32)]
        out_specs += [prompt_spec, sample_spec]
    elif want_x:
        out_shape.append(jax.ShapeDtypeStruct((rows, d), F32))
        out_specs.append(row_spec)
    if want_n:
        out_shape.append(jax.ShapeDtypeStruct((rows, d), BF16))
        out_specs.append(row_spec)
    return pl.pallas_call(
        functools.partial(_resnorm_kernel, alpha=alpha, has_y=has_y, want_x=want_x, want_n=want_n,
                          two_in=two_in, two_out=two_out, n_ptiles=n_ptiles, n_ytiles=n_ytiles),
        grid=(rows // tr,), in_specs=in_specs, out_specs=out_specs, out_shape=out_shape,
        compiler_params=_cparams(1), name="resnorm")(*args)


def _ffn_kernel(*refs, n_side, side_steps, emit_bf16):
    xn_ref, wg_ref, wu_ref, wd_ref = refs[:4]
    side_in = refs[4:4 + n_side]
    o_ref = refs[4 + n_side]
    n_emit = 3 if emit_bf16 else 0
    emit_out = refs[5 + n_side:5 + n_side + n_emit]
    side_out = refs[5 + n_side + n_emit:]
    j = pl.program_id(1)

    @pl.when(j == 0)
    def _():
        o_ref[...] = jnp.zeros_like(o_ref)

    wg = wg_ref[...].astype(BF16)
    wu = wu_ref[...].astype(BF16)
    wd = wd_ref[...].astype(BF16)
    for dst, w in zip(emit_out, (wg, wu, wd)):
        dst[...] = w
    xn = xn_ref[...]
    g = jnp.dot(xn, wg, preferred_element_type=F32)
    u = jnp.dot(xn, wu, preferred_element_type=F32)
    h = (g * jax.nn.sigmoid(g) * u).astype(BF16)
    o_ref[...] += jnp.dot(h, wd, preferred_element_type=F32)

    if n_side:
        @pl.when(pl.program_id(0) * pl.num_programs(1) + j < side_steps)
        def _():
            for src, dst in zip(side_in, side_out):
                dst[...] = src[...].astype(BF16)


def _ffn(xn, wg, wu, wd, side=(), *, tile0=0, n_tiles=None, emit_bf16=False, xn_buffers=1, out_buffers=1,
         tm=ROW_TILE, tf=FF_TILE):
    d = xn.shape[1]
    f = wg.shape[1]
    nf = f // tf
    n_tiles = xn.shape[0] // tm - tile0 if n_tiles is None else n_tiles
    rows = n_tiles * tm
    n_steps = n_tiles * nf
    side_args, side_in_specs, side_out_specs, side_out_shape = [], [], [], []
    side_steps = max([nblk for _, _, nblk, _ in side], default=0)
    for w, rb, nblk, cols in side:
        assert rb * nblk <= w.shape[0] and nblk <= n_steps and rb % (2 * SUBLANES) == 0 and cols % LANES == 0
        spec = pl.BlockSpec((rb, cols), lambda i, j, nblk=nblk: (jnp.minimum(i * nf + j, nblk - 1), 0))
        side_args.append(w)
        side_in_specs.append(spec)
        side_out_specs.append(spec)
        side_out_shape.append(jax.ShapeDtypeStruct((rb * nblk, cols), BF16))
    w_specs = [pl.BlockSpec((d, tf), lambda i, j: (0, j)),
               pl.BlockSpec((d, tf), lambda i, j: (0, j)),
               pl.BlockSpec((tf, d), lambda i, j: (j, 0))]
    emit_specs = w_specs if emit_bf16 else []
    emit_shape = [jax.ShapeDtypeStruct(w.shape, BF16) for w in (wg, wu, wd)] if emit_bf16 else []
    outs = pl.pallas_call(
        functools.partial(_ffn_kernel, n_side=len(side), side_steps=side_steps, emit_bf16=emit_bf16),
        grid=(n_tiles, nf),
        in_specs=[pl.BlockSpec((tm, d), lambda i, j: (tile0 + i, 0), pipeline_mode=pl.Buffered(xn_buffers))]
        + w_specs + side_in_specs,
        out_specs=[pl.BlockSpec((tm, d), lambda i, j: (i, 0), pipeline_mode=pl.Buffered(out_buffers))]
        + emit_specs + side_out_specs,
        out_shape=[jax.ShapeDtypeStruct((rows, d), F32)] + emit_shape + side_out_shape,
        compiler_params=_cparams(2), name="ffn")(xn, wg, wu, wd, *side_args)
    n_emit = len(emit_shape)
    return outs[0], outs[1:1 + n_emit], outs[1 + n_emit:]


def _mm_kernel(*refs, n_pairs, trans_w):
    o_ref = refs[-1]
    acc = None
    w_dim = 1 if trans_w else 0
    for p in range(n_pairs):
        d = lax.dot_general(refs[2 * p][...], refs[2 * p + 1][...].astype(BF16),
                            (((1,), (w_dim,)), ((), ())), preferred_element_type=F32)
        acc = d if acc is None else acc + d
    o_ref[...] = acc.astype(o_ref.dtype)


def _mm(pairs, *, trans_w=False, tm=ROW_TILE, tn=COL_TILE, out_dtype=F32):
    rows = pairs[0][0].shape[0]
    n = pairs[0][1].shape[0 if trans_w else 1]
    tn = min(tn, n)
    args, in_specs = [], []
    for a, w, kb in pairs:
        k = a.shape[1]
        args += [a, w]
        in_specs.append(pl.BlockSpec((tm, k), lambda i, j: (i, 0)))
        if trans_w:
            in_specs.append(pl.BlockSpec((tn, k), lambda i, j, kb=kb: (j, kb)))
        else:
            in_specs.append(pl.BlockSpec((k, tn), lambda i, j, kb=kb: (kb, j)))
    return pl.pallas_call(
        functools.partial(_mm_kernel, n_pairs=len(pairs), trans_w=trans_w),
        grid=(rows // tm, n // tn),
        in_specs=in_specs,
        out_specs=pl.BlockSpec((tm, tn), lambda i, j: (i, j)),
        out_shape=jax.ShapeDtypeStruct((rows, n), out_dtype),
        compiler_params=_cparams(2), name="mm")(*args)


def _cumsum_rows(x):
    t = x.shape[0]
    row = lax.broadcasted_iota(jnp.int32, x.shape, 0)
    k = 1
    while k < t:
        x = x + jnp.where(row >= k, pltpu.roll(x, k, axis=0), 0.0)
        k *= 2
    return x


def _mlstm_kernel(um_ref, v_ref, o_ref, gt_ref, cbuf_ref, c0_ref, n0_ref, m0_ref,
                  convw_ref, convb_ref, wq_ref, wk_ref, gbias_ref, hn_ref,
                  hm_ref, cout_ref, nout_ref, mout_ref,
                  xp_s, c_s, n_s, m_s, *, t, heads, dqk, dv, n_chunks):
    c = pl.program_id(1)

    @pl.when(c == 0)
    def _():
        xp_s[0:8, :] = cbuf_ref[0]
        c_s[...] = c0_ref[0]
        n_s[...] = n0_ref[0]
        m_s[...] = m0_ref[0]

    xp_s[8:8 + t, :] = um_ref[...]
    cv = convb_ref[...] + xp_s[8 - (CONV_W - 1):8 - (CONV_W - 1) + t, :] * convw_ref[0:1, :]
    for j in range(1, CONV_W):
        off = 8 - (CONV_W - 1) + j
        cv = cv + xp_s[off:off + t, :] * convw_ref[j:j + 1, :]
    cv = cv * jax.nn.sigmoid(cv)
    xp_s[0:8, :] = xp_s[t:t + 8, :]

    gt = gt_ref[...] + gbias_ref[...]
    lane = lax.broadcasted_iota(jnp.int32, gt.shape, 1)
    lf = jnp.minimum(gt, 0.0) - jnp.log1p(jnp.exp(-jnp.abs(gt)))
    g_all = jnp.where(lane < heads, gt, lf)
    b_all = _cumsum_rows(g_all)
    mix = jnp.where(lane < heads, g_all, b_all)
    tp = max(t, LANES)
    if tp != t:
        mix_p = jnp.concatenate([mix, jnp.zeros((tp - t, mix.shape[1]), F32)], axis=0)
    else:
        mix_p = mix
    mix_t = mix_p.T[:, :t]

    rr = lax.broadcasted_iota(jnp.int32, (t, t), 0)
    cc = lax.broadcasted_iota(jnp.int32, (t, t), 1)
    causal = cc <= rr

    for h in range(heads):
        ig_c = mix[:, h:h + 1]
        b_c = mix[:, heads + h:heads + h + 1]
        ig_r = mix_t[h:h + 1, :]
        b_r = mix_t[heads + h:heads + h + 1, :]
        m0 = m_s[h:h + 1, 0:1]
        log_d = jnp.where(causal, b_c - b_r + ig_r, -jnp.inf)
        inter = b_c + m0
        m_t = jnp.maximum(inter, jnp.max(log_d, axis=1, keepdims=True))
        dmat = jnp.exp(log_d - m_t)
        cvh = cv[:, h * dv:(h + 1) * dv].astype(BF16)
        q = jnp.dot(cvh, wq_ref[h], preferred_element_type=F32)
        k = jnp.dot(cvh, wk_ref[h], preferred_element_type=F32) * (dqk ** -0.5)
        qb = q.astype(BF16)
        kb = k.astype(BF16)
        s = lax.dot_general(qb, kb, (((1,), (1,)), ((), ())), preferred_element_type=F32) * dmat
        w_inter = jnp.exp(inter - m_t)
        c0 = c_s[h]
        n0 = n_s[h:h + 1, :]
        vh = v_ref[:, h * dv:(h + 1) * dv]
        num = (w_inter * jnp.dot(qb, c0.astype(BF16), preferred_element_type=F32)
               + jnp.dot(s.astype(BF16), vh.astype(BF16), preferred_element_type=F32))
        den = w_inter * jnp.sum(q * n0, axis=1, keepdims=True) + jnp.sum(s, axis=1, keepdims=True)
        hh = num / jnp.maximum(jnp.abs(den), jnp.exp(-m_t))
        b_last = b_c[t - 1:t, :]
        log_w = b_last - b_c + ig_c
        m_new = jnp.maximum(b_last + m0, jnp.max(log_w, axis=0, keepdims=True))
        w_col = jnp.exp(log_w - m_new)
        decay = jnp.exp(b_last + m0 - m_new)
        wv = (w_col * vh).astype(BF16)
        c_s[h] = decay * c0 + lax.dot_general(kb, wv, (((0,), (0,)), ((), ())),
                                              preferred_element_type=F32)
        n_s[h:h + 1, :] = decay * n0 + jnp.sum(w_col * k, axis=0, keepdims=True)
        m_s[h:h + 1, :] = jnp.broadcast_to(m_new, (1, m_s.shape[1]))
        hn = _rms(hh, hn_ref[h:h + 1, :])
        og = jax.nn.sigmoid(o_ref[:, h * dv:(h + 1) * dv])
        hm_ref[:, h * dv:(h + 1) * dv] = (hn * og).astype(BF16)

    @pl.when(c == n_chunks - 1)
    def _():
        cout_ref[0] = c_s[...]
        nout_ref[0] = n_s[...]
        mout_ref[0] = m_s[...]


def _mlstm(p, gates, row0, n_streams, n_chunks, t, cbuf, c0, n0, m0,
           convw, convb, wq, wk, gbias, hnorm):
    heads, dv, dqk = wq.shape
    dm = heads * dv
    blk0 = row0 // t
    assert row0 % t == 0

    def rows(b, c):
        return blk0 + b * n_chunks + c

    kern = functools.partial(_mlstm_kernel, t=t, heads=heads, dqk=dqk, dv=dv, n_chunks=n_chunks)
    n_rows = n_streams * n_chunks * t
    outs = pl.pallas_call(
        kern,
        grid=(n_streams, n_chunks),
        in_specs=[pl.BlockSpec((t, dm), lambda b, c: (rows(b, c), 0)),
                  pl.BlockSpec((t, dm), lambda b, c: (rows(b, c), 1)),
                  pl.BlockSpec((t, dm), lambda b, c: (rows(b, c), 2)),
                  pl.BlockSpec((t, LANES), lambda b, c: (rows(b, c), 0)),
                  pl.BlockSpec((1, 8, dm), lambda b, c: (b, 0, 0)),
                  pl.BlockSpec((1, heads, dqk, dv), lambda b, c: (b, 0, 0, 0)),
                  pl.BlockSpec((1, 8, dqk), lambda b, c: (b, 0, 0)),
                  pl.BlockSpec((1, SUBLANES, LANES), lambda b, c: (b, 0, 0)),
                  pl.BlockSpec((CONV_W, dm), lambda b, c: (0, 0)),
                  pl.BlockSpec((1, dm), lambda b, c: (0, 0)),
                  pl.BlockSpec((heads, dv, dqk), lambda b, c: (0, 0, 0)),
                  pl.BlockSpec((heads, dv, dqk), lambda b, c: (0, 0, 0)),
                  pl.BlockSpec((1, LANES), lambda b, c: (0, 0)),
                  pl.BlockSpec((heads, dv), lambda b, c: (0, 0))],
        out_specs=[pl.BlockSpec((t, dm), lambda b, c: (b * n_chunks + c, 0)),
                   pl.BlockSpec((1, heads, dqk, dv), lambda b, c: (b, 0, 0, 0)),
                   pl.BlockSpec((1, 8, dqk), lambda b, c: (b, 0, 0)),
                   pl.BlockSpec((1, SUBLANES, LANES), lambda b, c: (b, 0, 0))],
        out_shape=[jax.ShapeDtypeStruct((n_rows, dm), BF16),
                   jax.ShapeDtypeStruct((n_streams, heads, dqk, dv), F32),
                   jax.ShapeDtypeStruct((n_streams, 8, dqk), F32),
                   jax.ShapeDtypeStruct((n_streams, SUBLANES, LANES), F32)],
        scratch_shapes=[pltpu.VMEM((t + 8, dm), F32),
                        pltpu.VMEM((heads, dqk, dv), F32),
                        pltpu.VMEM((8, dqk), F32),
                        pltpu.VMEM((SUBLANES, LANES), F32)],
        compiler_params=_cparams(2), name="mlstm")(
            p, p, p, gates, cbuf, c0, n0, m0, convw, convb, wq, wk, gbias, hnorm)
    return outs


def _s5_kernel(u_ref, ax_re_ref, ax_im_ref, ldtx_ref, bt_re_ref, bt_im_ref, ct_re_ref, ct_im_ref,
               c2_re_ref, c2_im_ref, arow_re_ref, arow_im_ref, ldtrow_ref, d_ref, h0_re_ref, h0_im_ref,
               y_ref, hp_re_ref, hp_im_ref, hs_re_ref, hs_im_ref,
               h_re_s, h_im_s, bm_re_s, bm_im_s, cd_re_s, cd_im_s, cdat_re_s, cdat_im_s, k0_s,
               *, n_prompt, n_win):
    win, grp, st = S5_WIN, S5_GROUP, S5_STATE
    nc = u_ref.shape[1]
    nl = h_re_s.shape[1]

    dt = jnp.exp(ldtx_ref[...])
    ar = ax_re_ref[...]
    ai = ax_im_ref[...]
    mag = jnp.exp(ar * dt)
    ab_re = mag * jnp.cos(ai * dt)
    ab_im = mag * jnp.sin(ai * dt)
    den = ar * ar + ai * ai
    inv_re = ar / den
    inv_im = -ai / den
    f_re = ab_re - 1.0
    f_im = ab_im
    z_re = f_re * inv_re - f_im * inv_im
    z_im = f_re * inv_im + f_im * inv_re
    bb_re = z_re * bt_re_ref[...] - z_im * bt_im_ref[...]
    bb_im = z_re * bt_im_ref[...] + z_im * bt_re_ref[...]
    own_b = (lax.broadcasted_iota(jnp.int32, (nc, nl), 0) // grp
             == lax.broadcasted_iota(jnp.int32, (nc, nl), 1) // st)
    reps_b = nl // bb_re.shape[1]
    bd_re = jnp.where(own_b, jnp.concatenate([bb_re] * reps_b, axis=1), 0.0)
    bd_im = jnp.where(own_b, jnp.concatenate([bb_im] * reps_b, axis=1), 0.0)
    own_c =(lax.broadcasted_iota(jnp.int32, (nl, nc), 0) // st
             == lax.broadcasted_iota(jnp.int32, (nl, nc), 1) // grp)
    reps_c = nc // ct_re_ref.shape[1]
    cd_re_s[...] = jnp.where(own_c, jnp.concatenate([ct_re_ref[...]] * reps_c, axis=1), 0.0).astype(BF16)
    cd_im_s[...] = jnp.where(own_c, -jnp.concatenate([ct_im_ref[...]] * reps_c, axis=1), 0.0).astype(BF16)

    dtr = jnp.exp(ldtrow_ref[...])
    magr = jnp.exp(arow_re_ref[...] * dtr)
    a_re = magr * jnp.cos(arow_im_ref[...] * dtr)
    a_im = magr * jnp.sin(arow_im_ref[...] * dtr)
    aw_re, aw_im = a_re, a_im
    for _ in range(win.bit_length() - 1):
        aw_re, aw_im = aw_re * aw_re - aw_im * aw_im, 2.0 * aw_re * aw_im

    pw_re = jnp.ones_like(a_re)
    pw_im = jnp.zeros_like(a_im)
    for s in range(win - 1, -1, -1):
        bm_re_s[s * nc:(s + 1) * nc, :] = (bd_re * pw_re - bd_im * pw_im).astype(BF16)
        bm_im_s[s * nc:(s + 1) * nc, :] = (bd_re * pw_im + bd_im * pw_re).astype(BF16)
        pw_re, pw_im = pw_re * a_re - pw_im * a_im, pw_re * a_im + pw_im * a_re

    ct2_re = jnp.where(own_b, jnp.concatenate([c2_re_ref[...]] * reps_b, axis=1), 0.0)
    ct2_im = jnp.where(own_b, -jnp.concatenate([c2_im_ref[...]] * reps_b, axis=1), 0.0)
    cdat_re_s[...] = (ct2_re * a_re + ct2_im * a_im).astype(BF16)
    cdat_im_s[...] = (ct2_im * a_re - ct2_re * a_im).astype(BF16)
    k0_s[...] = (jnp.dot(bm_re_s[(win - 1) * nc:, :], cd_re_s[...], preferred_element_type=F32)
                 + jnp.dot(bm_im_s[(win - 1) * nc:, :], cd_im_s[...], preferred_element_type=F32)).astype(BF16)

    def slab(s):
        return u_ref[pl.ds(s, n_win, stride=win), :]

    u_all = jnp.concatenate([slab(s).astype(BF16) for s in range(win)], axis=1)
    h_re_s[...] = jnp.dot(u_all, bm_re_s[...], preferred_element_type=F32)
    h_im_s[...] = jnp.dot(u_all, bm_im_s[...], preferred_element_type=F32)

    h0r = h0_re_ref[...]
    h0i = h0_im_ref[...]
    hs_re_ref[...] = aw_re * h0r - aw_im * h0i + h_re_s[n_prompt:n_win, :]
    hs_im_ref[...] = aw_re * h0i + aw_im * h0r + h_im_s[n_prompt:n_win, :]
    h_re_s[n_prompt:n_win, :] = h0r
    h_im_s[n_prompt:n_win, :] = h0i

    def carry_step(c, carry):
        hr, hi = carry
        lr = h_re_s[pl.ds(c, 1), :]
        li = h_im_s[pl.ds(c, 1), :]
        h_re_s[pl.ds(c, 1), :] = hr
        h_im_s[pl.ds(c, 1), :] = hi
        return aw_re * hr - aw_im * hi + lr, aw_re * hi + aw_im * hr + li

    zero = jnp.zeros((1, nl), F32)
    hr, hi = lax.fori_loop(0, n_prompt, carry_step, (zero, zero))
    hp_re_ref[...] = jnp.broadcast_to(hr, hp_re_ref.shape)
    hp_im_ref[...] = jnp.broadcast_to(hi, hp_im_ref.shape)

    a2_re = a_re * a_re - a_im * a_im
    a2_im = 2.0 * a_re * a_im
    nt_dims = (((1,), (1,)), ((), ()))

    def step2(q, carry):
        xa = slab(2 * q)
        xb = slab(2 * q + 1)
        xa16 = xa.astype(BF16)
        hr = h_re_s[...]
        hi = h_im_s[...]
        y_a = (lax.dot_general(hr.astype(BF16), cdat_re_s[...], nt_dims, preferred_element_type=F32)
               + lax.dot_general(hi.astype(BF16), cdat_im_s[...], nt_dims, preferred_element_type=F32)
               + jnp.dot(xa16, k0_s[...], preferred_element_type=F32)
               + d_ref[...] * xa)
        y_ref[pl.ds(2 * q, n_win, stride=win), :] = y_a
        xab = jnp.concatenate([xa16, xb.astype(BF16)], axis=1)
        nr = a2_re * hr - a2_im * hi + jnp.dot(xab, bm_re_s[(win - 2) * nc:, :], preferred_element_type=F32)
        ni = a2_re * hi + a2_im * hr + jnp.dot(xab, bm_im_s[(win - 2) * nc:, :], preferred_element_type=F32)
        h_re_s[...] = nr
        h_im_s[...] = ni
        y_b = (jnp.dot(nr.astype(BF16), cd_re_s[...], preferred_element_type=F32)
               + jnp.dot(ni.astype(BF16), cd_im_s[...], preferred_element_type=F32)
               + d_ref[...] * xb)
        y_ref[pl.ds(2 * q + 1, n_win, stride=win), :] = y_b
        return carry

    lax.fori_loop(0, win // 2, step2, 0, unroll=2)


def _s5(proj, col0, a_re, a_im, log_dt, b_re, b_im, c_re, c_im, d, h0_re, h0_im, *, n_prompt_rows):
    groups, st = a_re.shape
    grp, win, nc = S5_GROUP, S5_WIN, S5_TILE
    gt = nc // grp
    nl = gt * st
    tiles = groups // gt
    rows = proj.shape[0]
    n_win = rows // win
    n_prompt = n_prompt_rows // win
    n_s = n_win - n_prompt
    assert col0 % nc == 0 and (groups * grp) % nc == 0 and win & (win - 1) == 0

    dup = lambda x: jnp.concatenate([x, x], axis=-1)
    rep = lambda x: jnp.repeat(x, grp, axis=0)
    ax_re = rep(dup(a_re))
    ax_im = rep(dup(a_im))
    ldtx = jnp.broadcast_to(rep(log_dt[:, None]), (groups * grp, 2 * st))
    bt_re = dup(jnp.swapaxes(b_re, 1, 2)).reshape(groups * grp, 2 * st)
    bt_im = dup(jnp.swapaxes(b_im, 1, 2)).reshape(groups * grp, 2 * st)
    ct_re = jnp.tile(jnp.swapaxes(c_re, 1, 2), (1, 1, LANES // grp)).reshape(groups * st, LANES)
    ct_im = jnp.tile(jnp.swapaxes(c_im, 1, 2), (1, 1, LANES // grp)).reshape(groups * st, LANES)
    c2_re = dup(c_re).reshape(groups * grp, 2 * st)
    c2_im = dup(c_im).reshape(groups * grp, 2 * st)
    arow_re = a_re.reshape(tiles, 1, nl)
    arow_im = a_im.reshape(tiles, 1, nl)
    ldtrow = jnp.repeat(log_dt, st).reshape(tiles, 1, nl)
    drow = d.reshape(tiles, 1, nc)

    rowblk = lambda w: pl.BlockSpec((nc, w), lambda j: (j, 0))
    vec = lambda w: pl.BlockSpec((None, 1, w), lambda j: (j, 0, 0))
    lanes = groups * st
    kern = functools.partial(_s5_kernel, n_prompt=n_prompt, n_win=n_win)
    return pl.pallas_call(
        kern, grid=(tiles,),
        in_specs=[pl.BlockSpec((rows, nc), lambda j: (0, col0 // nc + j)),
                  rowblk(2 * st), rowblk(2 * st), rowblk(2 * st), rowblk(2 * st), rowblk(2 * st),
                  pl.BlockSpec((nl, LANES), lambda j: (j, 0)), pl.BlockSpec((nl, LANES), lambda j: (j, 0)),
                  rowblk(2 * st), rowblk(2 * st),
                  vec(nl), vec(nl), vec(nl), vec(nc),
                  pl.BlockSpec((n_s, nl), lambda j: (0, j)), pl.BlockSpec((n_s, nl), lambda j: (0, j))],
        out_specs=[pl.BlockSpec((rows, nc), lambda j: (0, j)),
                   pl.BlockSpec((8, nl), lambda j: (0, j)), pl.BlockSpec((8, nl), lambda j: (0, j)),
                   pl.BlockSpec((n_s, nl), lambda j: (0, j)), pl.BlockSpec((n_s, nl), lambda j: (0, j))],
        out_shape=[jax.ShapeDtypeStruct((rows, groups * grp), F32),
                   jax.ShapeDtypeStruct((8, lanes), F32), jax.ShapeDtypeStruct((8, lanes), F32),
                   jax.ShapeDtypeStruct((n_s, lanes), F32), jax.ShapeDtypeStruct((n_s, lanes), F32)],
        scratch_shapes=[pltpu.VMEM((n_win, nl), F32), pltpu.VMEM((n_win, nl), F32),
                        pltpu.VMEM((win * nc, nl), BF16), pltpu.VMEM((win * nc, nl), BF16),
                        pltpu.VMEM((nl, nc), BF16), pltpu.VMEM((nl, nc), BF16),
                        pltpu.VMEM((nc, nl), BF16), pltpu.VMEM((nc, nl), BF16), pltpu.VMEM((nc, nc), BF16)],
        compiler_params=_cparams(1), name="s5")(
            proj, ax_re, ax_im, ldtx, bt_re, bt_im, ct_re, ct_im, c2_re, c2_im,
            arow_re, arow_im, ldtrow, drow, h0_re, h0_im)


def _glu_kernel(y_ref, w_ref, b_ref, o_ref):
    g = jax.nn.gelu(y_ref[...])
    z = jnp.dot(g.astype(BF16), w_ref[...], preferred_element_type=F32) + b_ref[...]
    o_ref[...] = (g * jax.nn.sigmoid(z)).astype(BF16)


def _glu(y, w, b, *, tm=ROW_TILE):
    rows, d = y.shape
    return pl.pallas_call(
        _glu_kernel, grid=(rows // tm,),
        in_specs=[pl.BlockSpec((tm, d), lambda i: (i, 0)),
                  pl.BlockSpec((d, d), lambda i: (0, 0)),
                  pl.BlockSpec((1, d), lambda i: (0, 0))],
        out_specs=pl.BlockSpec((tm, d), lambda i: (i, 0)),
        out_shape=jax.ShapeDtypeStruct((rows, d), BF16),
        compiler_params=_cparams(1), name="glu")(y, w, b.reshape(1, d))


def _pad_rows8(x, at_end=False):
    b, r, n = x.shape
    z = jnp.zeros((b, 8 - r, n), x.dtype)
    return jnp.concatenate([x, z] if not at_end else [z, x], axis=1)


def kernel(x_prompt, x_sample, state_mlstm_C, state_mlstm_n, state_mlstm_m, cache_mlstm_conv,
           state_s5_re, state_s5_im, ff1_norm_pre, ff1_norm_post, ff1_w_gate, ff1_w_up, ff1_w_down,
           mix_norm_pre, w_in, mlstm_conv_w, mlstm_conv_b, mlstm_w_q, mlstm_w_k, mlstm_b_i, mlstm_b_f,
           mlstm_head_norm, s5_a_re, s5_a_im, s5_log_dt, s5_b_re, s5_b_im, s5_c_re, s5_c_im, s5_d,
           s5_w_glu, s5_b_glu, w_out, mix_norm_post, ff2_norm_pre, ff2_norm_post, ff2_w_gate, ff2_w_up,
           ff2_w_down):
    depth = w_in.shape[0]
    bp, lp, d = x_prompt.shape
    bs, ls, _ = x_sample.shape
    assert bp == 1 and ls == S5_WIN and lp % MLSTM_CHUNK == 0
    heads, dv, dqk = mlstm_w_q.shape[1:]
    dm = heads * dv
    groups, st = s5_a_re.shape[1:]
    n_p = bp * lp
    n_s = bs * ls
    t_prompt = MLSTM_CHUNK

    x = (x_prompt.reshape(n_p, d), x_sample.reshape(n_s, d))
    new_p, new_s = [], []
    for l in range(depth):
        bf = lambda w: w[l].astype(BF16)
        last = l == depth - 1
        n_main = 3 * dm + groups * S5_GROUP
        xn = _resnorm(x, g_pre=ff1_norm_pre[l], want_x=False, n_prompt=n_p)[0]
        w_in_t = jnp.swapaxes(w_in[l], 0, 1)
        d_ff = ff2_w_gate.shape[2]
        later = [(ff2_w_gate[l], 16, d // 16, d_ff), (ff2_w_up[l], 16, d // 16, d_ff),
                 (ff2_w_down[l], 64, d_ff // 64, d), (w_in_t, 32, n_main // 32, d),
                 (w_out[l], 16, d // 16, d), (s5_w_glu[l], 16, s5_w_glu.shape[1] // 16, s5_w_glu.shape[2])]
        y_a, (wg1, wu1, wd1), _ = _ffn(xn, ff1_w_gate[l], ff1_w_up[l], ff1_w_down[l],
                                       n_tiles=1, emit_bf16=True)
        y_b, _, (wg2, wu2, wd2, w_in_b, w_out_b, w_glu_b) = _ffn(xn, wg1, wu1, wd1, side=later, tile0=1,
                                                                 xn_buffers=2)
        x, hmix = _resnorm(x, (y_a, y_b), ff1_norm_post[l], mix_norm_pre[l], alpha=0.5, n_prompt=n_p)

        proj = _mm([(hmix, w_in_b, 0)], trans_w=True, tm=MM_ROW_TILE)
        w_gate_rows = jnp.pad(w_in_t[n_main:], ((0, 128 - 2 * heads), (0, 0)))
        gates = _mm([(hmix, w_gate_rows, 0)], trans_w=True)
        gbias = jnp.pad(jnp.concatenate([mlstm_b_i[l], mlstm_b_f[l]]), (0, LANES - 2 * heads)).reshape(1, LANES)

        wq = bf(mlstm_w_q)
        wk = bf(mlstm_w_k)
        convw = mlstm_conv_w[l]
        convb = mlstm_conv_b[l].reshape(1, dm)
        hnorm = mlstm_head_norm[l]
        zc = jnp.zeros((bp, heads, dqk, dv), F32)
        hm_p, c_p, nn_p, m_p = _mlstm(
            proj, gates, 0, bp, lp // t_prompt, t_prompt,
            jnp.zeros((bp, SUBLANES, dm), F32), zc, jnp.zeros((bp, SUBLANES, dqk), F32),
            jnp.zeros((bp, SUBLANES, LANES), F32),
            convw, convb, wq, wk, gbias, hnorm)
        m0_s = jnp.broadcast_to(_pad_rows8(state_mlstm_m[l][:, :, None]), (bs, SUBLANES, LANES))
        hm_s, c_s, nn_s, m_s = _mlstm(
            proj, gates, n_p, bs, 1, ls,
            _pad_rows8(cache_mlstm_conv[l], at_end=True), state_mlstm_C[l],
            _pad_rows8(state_mlstm_n[l]), m0_s, convw, convb, wq, wk, gbias, hnorm)
        hm = jnp.concatenate([hm_p, hm_s], axis=0)
        keep = CONV_W - 1
        conv_p = proj[n_p - keep:n_p, :dm].reshape(bp, keep, dm)
        conv_s = proj[n_p:].reshape(bs, ls, -1)[:, ls - keep:, :dm]

        ys, hp_re, hp_im, hs_re, hs_im = _s5(
            proj, 3 * dm, s5_a_re[l], s5_a_im[l], s5_log_dt[l], s5_b_re[l], s5_b_im[l],
            s5_c_re[l], s5_c_im[l], s5_d[l],
            state_s5_re[l].reshape(bs, groups * st), state_s5_im[l].reshape(bs, groups * st),
            n_prompt_rows=n_p)
        ys = _glu(ys, w_glu_b, s5_b_glu[l])

        assert dm == ys.shape[1]
        mix = _mm([(hm, w_out_b, 0), (ys, w_out_b, 1)], tm=MM_ROW_TILE)
        x, xn = _resnorm(x, mix, mix_norm_post[l], ff2_norm_pre[l], alpha=1.0)
        y = _ffn(xn, wg2, wu2, wd2, xn_buffers=2, out_buffers=2)[0]
        x = _resnorm(x, y, ff2_norm_post[l], alpha=0.5, two_out=last, n_prompt=n_p)
        x = tuple(x) if last else (x[0][:n_p], x[0][n_p:])

        new_p.append((c_p, nn_p[:, :heads], m_p[:, :heads, 0], conv_p,
                      hp_re[:1].reshape(bp, groups, st), hp_im[:1].reshape(bp, groups, st)))
        new_s.append((c_s, nn_s[:, :heads], m_s[:, :heads, 0], conv_s,
                      hs_re.reshape(bs, groups, st), hs_im.reshape(bs, groups, st)))

    outs_p = [jnp.stack([e[i] for e in new_p]) for i in range(6)]
    outs_s = [jnp.stack([e[i] for e in new_s]) for i in range(6)]
    y_prompt = x[0].reshape(bp, lp, d)
    y_sample = x[1].reshape(bs, ls, d)
    return (y_prompt, y_sample, *outs_p, *outs_s)
```

```python
import functools

import jax
import jax.numpy as jnp
from jax import lax
from jax.experimental import pallas as pl
from jax.experimental.pallas import tpu as pltpu

F32 = jnp.float32
BF16 = jnp.bfloat16
RMS_EPS = 1e-6
CONV_W = 4
S5_WIN = 16
S5_GROUP = 16
S5_STATE = 64

V7X_VMEM_BYTES = 64 * 1024 * 1024
LANES = 128
SUBLANES = 8
VMEM_LIMIT = V7X_VMEM_BYTES * 7 // 8

S5_TILE = LANES
ROW_TILE = 768
MM_ROW_TILE = 1056
FF_TILE = 256
COL_TILE = 1024
NORM_ROWS = 256
MLSTM_CHUNK = 512


def _cparams(n_axes):
    return pltpu.CompilerParams(dimension_semantics=("arbitrary",) * n_axes,
                                vmem_limit_bytes=VMEM_LIMIT)


def _rms(v, g):
    return v * lax.rsqrt(jnp.mean(v * v, axis=-1, keepdims=True) + RMS_EPS) * g


def _resnorm_kernel(*refs, alpha, has_y, want_x, want_n, two_in, two_out, n_ptiles, n_ytiles):
    refs = list(refs)
    i = pl.program_id(0)
    if two_in:
        xp_ref = refs.pop(0)
        xs_ref = refs.pop(0)
        x = jnp.where(i < n_ptiles, xp_ref[...], xs_ref[...])
    else:
        x = refs.pop(0)[...]
    if has_y:
        if n_ytiles:
            ya_ref = refs.pop(0)
            yb_ref = refs.pop(0)
            y = jnp.where(i < n_ytiles, ya_ref[...], yb_ref[...])
        else:
            y = refs.pop(0)[...]
        gpost_ref = refs.pop(0)
        x = x + alpha * _rms(y, gpost_ref[...])
    if want_n:
        gpre_ref = refs.pop(0)
    if want_x and two_out:
        op_ref = refs.pop(0)
        os_ref = refs.pop(0)

        @pl.when(i < n_ptiles)
        def _():
            op_ref[...] = x

        @pl.when(i >= n_ptiles)
        def _():
            os_ref[...] = x
    elif want_x:
        refs.pop(0)[...] = x
    if want_n:
        refs.pop(0)[...] = _rms(x, gpre_ref[...]).astype(BF16)


def _resnorm(x, y=None, g_post=None, g_pre=None, *, alpha=1.0, want_x=True, two_out=False, n_prompt=0,
             tr=NORM_ROWS):
    two_in = isinstance(x, tuple)
    d = x[0].shape[1] if two_in else x.shape[1]
    rows = (x[0].shape[0] + x[1].shape[0]) if two_in else x.shape[0]
    n_ptiles = n_prompt // tr
    if two_in or two_out:
        assert n_prompt % tr == 0 and rows - n_prompt == tr
    has_y = y is not None
    want_n = g_pre is not None
    row_spec = pl.BlockSpec((tr, d), lambda i: (i, 0))
    vec_spec = pl.BlockSpec((1, d), lambda i: (0, 0))
    prompt_spec = pl.BlockSpec((tr, d), lambda i: (jnp.minimum(i, n_ptiles - 1), 0))
    sample_spec = pl.BlockSpec((tr, d), lambda i: (0, 0))
    if two_in:
        args, in_specs = [x[0], x[1]], [prompt_spec, sample_spec]
    else:
        args, in_specs = [x], [row_spec]
    n_ytiles = 0
    if has_y and isinstance(y, tuple):
        n_ytiles = y[0].shape[0] // tr
        assert y[0].shape[0] % tr == 0 and y[0].shape[0] + y[1].shape[0] == rows
        args += [y[0], y[1], g_post.reshape(1, d)]
        in_specs += [pl.BlockSpec((tr, d), lambda i: (jnp.minimum(i, n_ytiles - 1), 0)),
                     pl.BlockSpec((tr, d), lambda i: (jnp.maximum(i - n_ytiles, 0), 0)), vec_spec]
    elif has_y:
        args += [y, g_post.reshape(1, d)]
        in_specs += [row_spec, vec_spec]
    if want_n:
        args.append(g_pre.reshape(1, d))
        in_specs.append(vec_spec)
    out_shape, out_specs = [], []
    if want_x and two_out:
        out_shape += [jax.ShapeDtypeStruct((n_prompt, d), F32), jax.ShapeDtypeStruct((tr, d), F32)]
        out_specs += [prompt_spec, sample_spec]
    elif want_x:
        out_shape.append(jax.ShapeDtypeStruct((rows, d), F32))
        out_specs.append(row_spec)
    if want_n:
        out_shape.append(jax.ShapeDtypeStruct((rows, d), BF16))
        out_specs.append(row_spec)
    return pl.pallas_call(
        functools.partial(_resnorm_kernel, alpha=alpha, has_y=has_y, want_x=want_x, want_n=want_n,
                          two_in=two_in, two_out=two_out, n_ptiles=n_ptiles, n_ytiles=n_ytiles),
        grid=(rows // tr,), in_specs=in_specs, out_specs=out_specs, out_shape=out_shape,
        compiler_params=_cparams(1), name="resnorm")(*args)


def _ffn_kernel(*refs, n_side, side_steps, emit_bf16):
    xn_ref, wg_ref, wu_ref, wd_ref = refs[:4]
    side_in = refs[4:4 + n_side]
    o_ref = refs[4 + n_side]
    n_emit = 3 if emit_bf16 else 0
    emit_out = refs[5 + n_side:5 + n_side + n_emit]
    side_out = refs[5 + n_side + n_emit:]
    j = pl.program_id(1)

    @pl.when(j == 0)
    def _():
        o_ref[...] = jnp.zeros_like(o_ref)

    wg = wg_ref[...].astype(BF16)
    wu = wu_ref[...].astype(BF16)
    wd = wd_ref[...].astype(BF16)
    for dst, w in zip(emit_out, (wg, wu, wd)):
        dst[...] = w
    xn = xn_ref[...]
    g = jnp.dot(xn, wg, preferred_element_type=F32)
    u = jnp.dot(xn, wu, preferred_element_type=F32)
    h = (g * jax.nn.sigmoid(g) * u).astype(BF16)
    o_ref[...] += jnp.dot(h, wd, preferred_element_type=F32)

    if n_side:
        @pl.when(pl.program_id(0) * pl.num_programs(1) + j < side_steps)
        def _():
            for src, dst in zip(side_in, side_out):
                dst[...] = src[...].astype(BF16)


def _ffn(xn, wg, wu, wd, side=(), *, tile0=0, n_tiles=None, emit_bf16=False, xn_buffers=1, out_buffers=1,
         tm=ROW_TILE, tf=FF_TILE):
    d = xn.shape[1]
    f = wg.shape[1]
    nf = f // tf
    n_tiles = xn.shape[0] // tm - tile0 if n_tiles is None else n_tiles
    rows = n_tiles * tm
    n_steps = n_tiles * nf
    side_args, side_in_specs, side_out_specs, side_out_shape = [], [], [], []
    side_steps = max([nblk for _, _, nblk, _ in side], default=0)
    for w, rb, nblk, cols in side:
        assert rb * nblk <= w.shape[0] and nblk <= n_steps and rb % (2 * SUBLANES) == 0 and cols % LANES == 0
        spec = pl.BlockSpec((rb, cols), lambda i, j, nblk=nblk: (jnp.minimum(i * nf + j, nblk - 1), 0))
        side_args.append(w)
        side_in_specs.append(spec)
        side_out_specs.append(spec)
        side_out_shape.append(jax.ShapeDtypeStruct((rb * nblk, cols), BF16))
    w_specs = [pl.BlockSpec((d, tf), lambda i, j: (0, j)),
               pl.BlockSpec((d, tf), lambda i, j: (0, j)),
               pl.BlockSpec((tf, d), lambda i, j: (j, 0))]
    emit_specs = w_specs if emit_bf16 else []
    emit_shape = [jax.ShapeDtypeStruct(w.shape, BF16) for w in (wg, wu, wd)] if emit_bf16 else []
    outs = pl.pallas_call(
        functools.partial(_ffn_kernel, n_side=len(side), side_steps=side_steps, emit_bf16=emit_bf16),
        grid=(n_tiles, nf),
        in_specs=[pl.BlockSpec((tm, d), lambda i, j: (tile0 + i, 0), pipeline_mode=pl.Buffered(xn_buffers))]
        + w_specs + side_in_specs,
        out_specs=[pl.BlockSpec((tm, d), lambda i, j: (i, 0), pipeline_mode=pl.Buffered(out_buffers))]
        + emit_specs + side_out_specs,
        out_shape=[jax.ShapeDtypeStruct((rows, d), F32)] + emit_shape + side_out_shape,
        compiler_params=_cparams(2), name="ffn")(xn, wg, wu, wd, *side_args)
    n_emit = len(emit_shape)
    return outs[0], outs[1:1 + n_emit], outs[1 + n_emit:]


def _mm_kernel(*refs, n_pairs, trans_w, narrow):
    if narrow:
        wn_ref, o_ref, on_ref = refs[2 * n_pairs:]

        @pl.when(pl.program_id(1) == 0)
        def _():
            on_ref[...] = lax.dot_general(refs[0][...], wn_ref[...].astype(BF16),
                                          (((1,), (1,)), ((), ())), preferred_element_type=F32)
    else:
        o_ref = refs[-1]
    acc = None
    w_dim = 1 if trans_w else 0
    for p in range(n_pairs):
        d = lax.dot_general(refs[2 * p][...], refs[2 * p + 1][...].astype(BF16),
                            (((1,), (w_dim,)), ((), ())), preferred_element_type=F32)
        acc = d if acc is None else acc + d
    o_ref[...] = acc.astype(o_ref.dtype)


def _mm(pairs, *, trans_w=False, narrow_w=None, tm=ROW_TILE, tn=COL_TILE, out_dtype=F32):
    rows = pairs[0][0].shape[0]
    n = pairs[0][1].shape[0 if trans_w else 1]
    tn = min(tn, n)
    args, in_specs = [], []
    for a, w, kb in pairs:
        k = a.shape[1]
        args += [a, w]
        in_specs.append(pl.BlockSpec((tm, k), lambda i, j: (i, 0)))
        if trans_w:
            in_specs.append(pl.BlockSpec((tn, k), lambda i, j, kb=kb: (j, kb)))
        else:
            in_specs.append(pl.BlockSpec((k, tn), lambda i, j, kb=kb: (kb, j)))
    out_specs = [pl.BlockSpec((tm, tn), lambda i, j: (i, j))]
    out_shape = [jax.ShapeDtypeStruct((rows, n), out_dtype)]
    if narrow_w is not None:
        nn, kn = narrow_w.shape
        args.append(narrow_w)
        in_specs.append(pl.BlockSpec((nn, kn), lambda i, j: (0, 0)))
        out_specs.append(pl.BlockSpec((tm, nn), lambda i, j: (i, 0)))
        out_shape.append(jax.ShapeDtypeStruct((rows, nn), F32))
    outs = pl.pallas_call(
        functools.partial(_mm_kernel, n_pairs=len(pairs), trans_w=trans_w, narrow=narrow_w is not None),
        grid=(rows // tm, n // tn),
        in_specs=in_specs, out_specs=out_specs, out_shape=out_shape,
        compiler_params=_cparams(2), name="mm")(*args)
    return outs[0] if narrow_w is None else outs


def _cumsum_rows(x):
    t = x.shape[0]
    row = lax.broadcasted_iota(jnp.int32, x.shape, 0)
    k = 1
    while k < t:
        x = x + jnp.where(row >= k, pltpu.roll(x, k, axis=0), 0.0)
        k *= 2
    return x


def _mlstm_kernel(um_ref, v_ref, o_ref, gt_ref, cbuf_ref, c0_ref, n0_ref, m0_ref,
                  convw_ref, convb_ref, wq_ref, wk_ref, gbias_ref, hn_ref,
                  hm_ref, cout_ref, nout_ref, mout_ref,
                  xp_s, c_s, n_s, m_s, *, t, heads, dqk, dv, n_chunks):
    c = pl.program_id(1)

    @pl.when(c == 0)
    def _():
        xp_s[0:8, :] = cbuf_ref[0]
        c_s[...] = c0_ref[0]
        n_s[...] = n0_ref[0]
        m_s[...] = m0_ref[0]

    xp_s[8:8 + t, :] = um_ref[...]
    cv = convb_ref[...] + xp_s[8 - (CONV_W - 1):8 - (CONV_W - 1) + t, :] * convw_ref[0:1, :]
    for j in range(1, CONV_W):
        off = 8 - (CONV_W - 1) + j
        cv = cv + xp_s[off:off + t, :] * convw_ref[j:j + 1, :]
    cv = cv * jax.nn.sigmoid(cv)
    xp_s[0:8, :] = xp_s[t:t + 8, :]

    gt = gt_ref[...] + gbias_ref[...]
    lane = lax.broadcasted_iota(jnp.int32, gt.shape, 1)
    lf = jnp.minimum(gt, 0.0) - jnp.log1p(jnp.exp(-jnp.abs(gt)))
    g_all = jnp.where(lane < heads, gt, lf)
    b_all = _cumsum_rows(g_all)
    mix = jnp.where(lane < heads, g_all, b_all)
    tp = max(t, LANES)
    if tp != t:
        mix_p = jnp.concatenate([mix, jnp.zeros((tp - t, mix.shape[1]), F32)], axis=0)
    else:
        mix_p = mix
    mix_t = mix_p.T[:, :t]

    rr = lax.broadcasted_iota(jnp.int32, (t, t), 0)
    cc = lax.broadcasted_iota(jnp.int32, (t, t), 1)
    causal = cc <= rr

    for h in range(heads):
        ig_c = mix[:, h:h + 1]
        b_c = mix[:, heads + h:heads + h + 1]
        ig_r = mix_t[h:h + 1, :]
        b_r = mix_t[heads + h:heads + h + 1, :]
        m0 = m_s[h:h + 1, 0:1]
        log_d = jnp.where(causal, b_c - b_r + ig_r, -jnp.inf)
        inter = b_c + m0
        m_t = jnp.maximum(inter, jnp.max(log_d, axis=1, keepdims=True))
        dmat = jnp.exp(log_d - m_t)
        cvh = cv[:, h * dv:(h + 1) * dv].astype(BF16)
        q = jnp.dot(cvh, wq_ref[h], preferred_element_type=F32)
        k = jnp.dot(cvh, wk_ref[h], preferred_element_type=F32) * (dqk ** -0.5)
        qb = q.astype(BF16)
        kb = k.astype(BF16)
        s = lax.dot_general(qb, kb, (((1,), (1,)), ((), ())), preferred_element_type=F32) * dmat
        w_inter = jnp.exp(inter - m_t)
        c0 = c_s[h]
        n0 = n_s[h:h + 1, :]
        vh = v_ref[:, h * dv:(h + 1) * dv]
        num = (w_inter * jnp.dot(qb, c0.astype(BF16), preferred_element_type=F32)
               + jnp.dot(s.astype(BF16), vh.astype(BF16), preferred_element_type=F32))
        den = w_inter * jnp.sum(q * n0, axis=1, keepdims=True) + jnp.sum(s, axis=1, keepdims=True)
        hh = num / jnp.maximum(jnp.abs(den), jnp.exp(-m_t))
        b_last = b_c[t - 1:t, :]
        log_w = b_last - b_c + ig_c
        m_new = jnp.maximum(b_last + m0, jnp.max(log_w, axis=0, keepdims=True))
        w_col = jnp.exp(log_w - m_new)
        decay = jnp.exp(b_last + m0 - m_new)
        wv = (w_col * vh).astype(BF16)
        c_s[h] = decay * c0 + lax.dot_general(kb, wv, (((0,), (0,)), ((), ())),
                                              preferred_element_type=F32)
        n_s[h:h + 1, :] = decay * n0 + jnp.sum(w_col * k, axis=0, keepdims=True)
        m_s[h:h + 1, :] = jnp.broadcast_to(m_new, (1, m_s.shape[1]))
        hn = _rms(hh, hn_ref[h:h + 1, :])
        og = jax.nn.sigmoid(o_ref[:, h * dv:(h + 1) * dv])
        hm_ref[:, h * dv:(h + 1) * dv] = (hn * og).astype(BF16)

    @pl.when(c == n_chunks - 1)
    def _():
        cout_ref[0] = c_s[...]
        nout_ref[0] = n_s[...]
        mout_ref[0] = m_s[...]


def _mlstm(p, gates, row0, n_streams, n_chunks, t, cbuf, c0, n0, m0,
           convw, convb, wq, wk, gbias, hnorm):
    heads, dv, dqk = wq.shape
    dm = heads * dv
    blk0 = row0 // t
    assert row0 % t == 0

    def rows(b, c):
        return blk0 + b * n_chunks + c

    kern = functools.partial(_mlstm_kernel, t=t, heads=heads, dqk=dqk, dv=dv, n_chunks=n_chunks)
    n_rows = n_streams * n_chunks * t
    outs = pl.pallas_call(
        kern,
        grid=(n_streams, n_chunks),
        in_specs=[pl.BlockSpec((t, dm), lambda b, c: (rows(b, c), 0)),
                  pl.BlockSpec((t, dm), lambda b, c: (rows(b, c), 1)),
                  pl.BlockSpec((t, dm), lambda b, c: (rows(b, c), 2)),
                  pl.BlockSpec((t, LANES), lambda b, c: (rows(b, c), 0)),
                  pl.BlockSpec((1, 8, dm), lambda b, c: (b, 0, 0)),
                  pl.BlockSpec((1, heads, dqk, dv), lambda b, c: (b, 0, 0, 0)),
                  pl.BlockSpec((1, 8, dqk), lambda b, c: (b, 0, 0)),
                  pl.BlockSpec((1, SUBLANES, LANES), lambda b, c: (b, 0, 0)),
                  pl.BlockSpec((CONV_W, dm), lambda b, c: (0, 0)),
                  pl.BlockSpec((1, dm), lambda b, c: (0, 0)),
                  pl.BlockSpec((heads, dv, dqk), lambda b, c: (0, 0, 0)),
                  pl.BlockSpec((heads, dv, dqk), lambda b, c: (0, 0, 0)),
                  pl.BlockSpec((1, LANES), lambda b, c: (0, 0)),
                  pl.BlockSpec((heads, dv), lambda b, c: (0, 0))],
        out_specs=[pl.BlockSpec((t, dm), lambda b, c: (b * n_chunks + c, 0)),
                   pl.BlockSpec((1, heads, dqk, dv), lambda b, c: (b, 0, 0, 0)),
                   pl.BlockSpec((1, 8, dqk), lambda b, c: (b, 0, 0)),
                   pl.BlockSpec((1, SUBLANES, LANES), lambda b, c: (b, 0, 0))],
        out_shape=[jax.ShapeDtypeStruct((n_rows, dm), BF16),
                   jax.ShapeDtypeStruct((n_streams, heads, dqk, dv), F32),
                   jax.ShapeDtypeStruct((n_streams, 8, dqk), F32),
                   jax.ShapeDtypeStruct((n_streams, SUBLANES, LANES), F32)],
        scratch_shapes=[pltpu.VMEM((t + 8, dm), F32),
                        pltpu.VMEM((heads, dqk, dv), F32),
                        pltpu.VMEM((8, dqk), F32),
                        pltpu.VMEM((SUBLANES, LANES), F32)],
        compiler_params=_cparams(2), name="mlstm")(
            p, p, p, gates, cbuf, c0, n0, m0, convw, convb, wq, wk, gbias, hnorm)
    return outs


def _s5_kernel(u_ref, ax_re_ref, ax_im_ref, ldtx_ref, bt_re_ref, bt_im_ref, ct_re_ref, ct_im_ref,
               c2_re_ref, c2_im_ref, arow_re_ref, arow_im_ref, ldtrow_ref, d_ref, h0_re_ref, h0_im_ref,
               y_ref, hp_re_ref, hp_im_ref, hs_re_ref, hs_im_ref,
               h_re_s, h_im_s, bm_re_s, bm_im_s, cd_re_s, cd_im_s, cdat_re_s, cdat_im_s, k0_s,
               *, n_prompt, n_win):
    win, grp, st = S5_WIN, S5_GROUP, S5_STATE
    nc = u_ref.shape[1]
    nl = h_re_s.shape[1]

    dt = jnp.exp(ldtx_ref[...])
    ar = ax_re_ref[...]
    ai = ax_im_ref[...]
    mag = jnp.exp(ar * dt)
    ab_re = mag * jnp.cos(ai * dt)
    ab_im = mag * jnp.sin(ai * dt)
    den = ar * ar + ai * ai
    inv_re = ar / den
    inv_im = -ai / den
    f_re = ab_re - 1.0
    f_im = ab_im
    z_re = f_re * inv_re - f_im * inv_im
    z_im = f_re * inv_im + f_im * inv_re
    bb_re = z_re * bt_re_ref[...] - z_im * bt_im_ref[...]
    bb_im = z_re * bt_im_ref[...] + z_im * bt_re_ref[...]
    own_b = (lax.broadcasted_iota(jnp.int32, (nc, nl), 0) // grp
             == lax.broadcasted_iota(jnp.int32, (nc, nl), 1) // st)
    reps_b = nl // bb_re.shape[1]
    bd_re = jnp.where(own_b, jnp.concatenate([bb_re] * reps_b, axis=1), 0.0)
    bd_im = jnp.where(own_b, jnp.concatenate([bb_im] * reps_b, axis=1), 0.0)
    own_c =(lax.broadcasted_iota(jnp.int32, (nl, nc), 0) // st
             == lax.broadcasted_iota(jnp.int32, (nl, nc), 1) // grp)
    reps_c = nc // ct_re_ref.shape[1]
    cd_re_s[...] = jnp.where(own_c, jnp.concatenate([ct_re_ref[...]] * reps_c, axis=1), 0.0).astype(BF16)
    cd_im_s[...] = jnp.where(own_c, -jnp.concatenate([ct_im_ref[...]] * reps_c, axis=1), 0.0).astype(BF16)

    dtr = jnp.exp(ldtrow_ref[...])
    magr = jnp.exp(arow_re_ref[...] * dtr)
    a_re = magr * jnp.cos(arow_im_ref[...] * dtr)
    a_im = magr * jnp.sin(arow_im_ref[...] * dtr)
    aw_re, aw_im = a_re, a_im
    for _ in range(win.bit_length() - 1):
        aw_re, aw_im = aw_re * aw_re - aw_im * aw_im, 2.0 * aw_re * aw_im

    pw_re = jnp.ones_like(a_re)
    pw_im = jnp.zeros_like(a_im)
    for s in range(win - 1, -1, -1):
        bm_re_s[s * nc:(s + 1) * nc, :] = (bd_re * pw_re - bd_im * pw_im).astype(BF16)
        bm_im_s[s * nc:(s + 1) * nc, :] = (bd_re * pw_im + bd_im * pw_re).astype(BF16)
        pw_re, pw_im = pw_re * a_re - pw_im * a_im, pw_re * a_im + pw_im * a_re

    ct2_re = jnp.where(own_b, jnp.concatenate([c2_re_ref[...]] * reps_b, axis=1), 0.0)
    ct2_im = jnp.where(own_b, -jnp.concatenate([c2_im_ref[...]] * reps_b, axis=1), 0.0)
    cdat_re_s[...] = (ct2_re * a_re + ct2_im * a_im).astype(BF16)
    cdat_im_s[...] = (ct2_im * a_re - ct2_re * a_im).astype(BF16)
    k0_s[...] = (jnp.dot(bm_re_s[(win - 1) * nc:, :], cd_re_s[...], preferred_element_type=F32)
                 + jnp.dot(bm_im_s[(win - 1) * nc:, :], cd_im_s[...], preferred_element_type=F32)).astype(BF16)

    def slab(s):
        return u_ref[pl.ds(s, n_win, stride=win), :]

    u_all = jnp.concatenate([slab(s).astype(BF16) for s in range(win)], axis=1)
    h_re_s[...] = jnp.dot(u_all, bm_re_s[...], preferred_element_type=F32)
    h_im_s[...] = jnp.dot(u_all, bm_im_s[...], preferred_element_type=F32)

    h0r = h0_re_ref[...]
    h0i = h0_im_ref[...]
    hs_re_ref[...] = aw_re * h0r - aw_im * h0i + h_re_s[n_prompt:n_win, :]
    hs_im_ref[...] = aw_re * h0i + aw_im * h0r + h_im_s[n_prompt:n_win, :]
    h_re_s[n_prompt:n_win, :] = h0r
    h_im_s[n_prompt:n_win, :] = h0i

    def carry_step(c, carry):
        hr, hi = carry
        lr = h_re_s[pl.ds(c, 1), :]
        li = h_im_s[pl.ds(c, 1), :]
        h_re_s[pl.ds(c, 1), :] = hr
        h_im_s[pl.ds(c, 1), :] = hi
        return aw_re * hr - aw_im * hi + lr, aw_re * hi + aw_im * hr + li

    zero = jnp.zeros((1, nl), F32)
    hr, hi = lax.fori_loop(0, n_prompt, carry_step, (zero, zero))
    hp_re_ref[...] = jnp.broadcast_to(hr, hp_re_ref.shape)
    hp_im_ref[...] = jnp.broadcast_to(hi, hp_im_ref.shape)

    a2_re = a_re * a_re - a_im * a_im
    a2_im = 2.0 * a_re * a_im
    nt_dims = (((1,), (1,)), ((), ()))

    def step2(q, carry):
        xa = slab(2 * q)
        xb = slab(2 * q + 1)
        xa16 = xa.astype(BF16)
        hr = h_re_s[...]
        hi = h_im_s[...]
        y_a = (lax.dot_general(hr.astype(BF16), cdat_re_s[...], nt_dims, preferred_element_type=F32)
               + lax.dot_general(hi.astype(BF16), cdat_im_s[...], nt_dims, preferred_element_type=F32)
               + jnp.dot(xa16, k0_s[...], preferred_element_type=F32)
               + d_ref[...] * xa)
        y_ref[pl.ds(2 * q, n_win, stride=win), :] = y_a
        xab = jnp.concatenate([xa16, xb.astype(BF16)], axis=1)
        nr = a2_re * hr - a2_im * hi + jnp.dot(xab, bm_re_s[(win - 2) * nc:, :], preferred_element_type=F32)
        ni = a2_re * hi + a2_im * hr + jnp.dot(xab, bm_im_s[(win - 2) * nc:, :], preferred_element_type=F32)
        h_re_s[...] = nr
        h_im_s[...] = ni
        y_b = (jnp.dot(nr.astype(BF16), cd_re_s[...], preferred_element_type=F32)
               + jnp.dot(ni.astype(BF16), cd_im_s[...], preferred_element_type=F32)
               + d_ref[...] * xb)
        y_ref[pl.ds(2 * q + 1, n_win, stride=win), :] = y_b
        return carry

    lax.fori_loop(0, win // 2, step2, 0, unroll=2)


def _s5(proj, col0, a_re, a_im, log_dt, b_re, b_im, c_re, c_im, d, h0_re, h0_im, *, n_prompt_rows):
    groups, st = a_re.shape
    grp, win, nc = S5_GROUP, S5_WIN, S5_TILE
    gt = nc // grp
    nl = gt * st
    tiles = groups // gt
    rows = proj.shape[0]
    n_win = rows // win
    n_prompt = n_prompt_rows // win
    n_s = n_win - n_prompt
    assert col0 % nc == 0 and (groups * grp) % nc == 0 and win & (win - 1) == 0

    dup = lambda x: jnp.concatenate([x, x], axis=-1)
    rep = lambda x: jnp.repeat(x, grp, axis=0)
    ax_re = rep(dup(a_re))
    ax_im = rep(dup(a_im))
    ldtx = jnp.broadcast_to(rep(log_dt[:, None]), (groups * grp, 2 * st))
    bt_re = dup(jnp.swapaxes(b_re, 1, 2)).reshape(groups * grp, 2 * st)
    bt_im = dup(jnp.swapaxes(b_im, 1, 2)).reshape(groups * grp, 2 * st)
    ct_re = jnp.tile(jnp.swapaxes(c_re, 1, 2), (1, 1, LANES // grp)).reshape(groups * st, LANES)
    ct_im = jnp.tile(jnp.swapaxes(c_im, 1, 2), (1, 1, LANES // grp)).reshape(groups * st, LANES)
    c2_re = dup(c_re).reshape(groups * grp, 2 * st)
    c2_im = dup(c_im).reshape(groups * grp, 2 * st)
    arow_re = a_re.reshape(tiles, 1, nl)
    arow_im = a_im.reshape(tiles, 1, nl)
    ldtrow = jnp.repeat(log_dt, st).reshape(tiles, 1, nl)
    drow = d.reshape(tiles, 1, nc)

    rowblk = lambda w: pl.BlockSpec((nc, w), lambda j: (j, 0))
    vec = lambda w: pl.BlockSpec((None, 1, w), lambda j: (j, 0, 0))
    lanes = groups * st
    kern = functools.partial(_s5_kernel, n_prompt=n_prompt, n_win=n_win)
    return pl.pallas_call(
        kern, grid=(tiles,),
        in_specs=[pl.BlockSpec((rows, nc), lambda j: (0, col0 // nc + j)),
                  rowblk(2 * st), rowblk(2 * st), rowblk(2 * st), rowblk(2 * st), rowblk(2 * st),
                  pl.BlockSpec((nl, LANES), lambda j: (j, 0)), pl.BlockSpec((nl, LANES), lambda j: (j, 0)),
                  rowblk(2 * st), rowblk(2 * st),
                  vec(nl), vec(nl), vec(nl), vec(nc),
                  pl.BlockSpec((n_s, nl), lambda j: (0, j)), pl.BlockSpec((n_s, nl), lambda j: (0, j))],
        out_specs=[pl.BlockSpec((rows, nc), lambda j: (0, j)),
                   pl.BlockSpec((8, nl), lambda j: (0, j)), pl.BlockSpec((8, nl), lambda j: (0, j)),
                   pl.BlockSpec((n_s, nl), lambda j: (0, j)), pl.BlockSpec((n_s, nl), lambda j: (0, j))],
        out_shape=[jax.ShapeDtypeStruct((rows, groups * grp), F32),
                   jax.ShapeDtypeStruct((8, lanes), F32), jax.ShapeDtypeStruct((8, lanes), F32),
                   jax.ShapeDtypeStruct((n_s, lanes), F32), jax.ShapeDtypeStruct((n_s, lanes), F32)],
        scratch_shapes=[pltpu.VMEM((n_win, nl), F32), pltpu.VMEM((n_win, nl), F32),
                        pltpu.VMEM((win * nc, nl), BF16), pltpu.VMEM((win * nc, nl), BF16),
                        pltpu.VMEM((nl, nc), BF16), pltpu.VMEM((nl, nc), BF16),
                        pltpu.VMEM((nc, nl), BF16), pltpu.VMEM((nc, nl), BF16), pltpu.VMEM((nc, nc), BF16)],
        compiler_params=_cparams(1), name="s5")(
            proj, ax_re, ax_im, ldtx, bt_re, bt_im, ct_re, ct_im, c2_re, c2_im,
            arow_re, arow_im, ldtrow, drow, h0_re, h0_im)


def _glu_kernel(y_ref, w_ref, b_ref, o_ref):
    g = jax.nn.gelu(y_ref[...])
    z = jnp.dot(g.astype(BF16), w_ref[...], preferred_element_type=F32) + b_ref[...]
    o_ref[...] = (g * jax.nn.sigmoid(z)).astype(BF16)


def _glu(y, w, b, *, tm=ROW_TILE):
    rows, d = y.shape
    return pl.pallas_call(
        _glu_kernel, grid=(rows // tm,),
        in_specs=[pl.BlockSpec((tm, d), lambda i: (i, 0)),
                  pl.BlockSpec((d, d), lambda i: (0, 0)),
                  pl.BlockSpec((1, d), lambda i: (0, 0))],
        out_specs=pl.BlockSpec((tm, d), lambda i: (i, 0)),
        out_shape=jax.ShapeDtypeStruct((rows, d), BF16),
        compiler_params=_cparams(1), name="glu")(y, w, b.reshape(1, d))


def _pad_rows8(x, at_end=False):
    b, r, n = x.shape
    z = jnp.zeros((b, 8 - r, n), x.dtype)
    return jnp.concatenate([x, z] if not at_end else [z, x], axis=1)


def kernel(x_prompt, x_sample, state_mlstm_C, state_mlstm_n, state_mlstm_m, cache_mlstm_conv,
           state_s5_re, state_s5_im, ff1_norm_pre, ff1_norm_post, ff1_w_gate, ff1_w_up, ff1_w_down,
           mix_norm_pre, w_in, mlstm_conv_w, mlstm_conv_b, mlstm_w_q, mlstm_w_k, mlstm_b_i, mlstm_b_f,
           mlstm_head_norm, s5_a_re, s5_a_im, s5_log_dt, s5_b_re, s5_b_im, s5_c_re, s5_c_im, s5_d,
           s5_w_glu, s5_b_glu, w_out, mix_norm_post, ff2_norm_pre, ff2_norm_post, ff2_w_gate, ff2_w_up,
           ff2_w_down):
    depth = w_in.shape[0]
    bp, lp, d = x_prompt.shape
    bs, ls, _ = x_sample.shape
    assert bp == 1 and ls == S5_WIN and lp % MLSTM_CHUNK == 0
    heads, dv, dqk = mlstm_w_q.shape[1:]
    dm = heads * dv
    groups, st = s5_a_re.shape[1:]
    n_p = bp * lp
    n_s = bs * ls
    t_prompt = MLSTM_CHUNK

    x = (x_prompt.reshape(n_p, d), x_sample.reshape(n_s, d))
    new_p, new_s = [], []
    for l in range(depth):
        bf = lambda w: w[l].astype(BF16)
        last = l == depth - 1
        n_main = 3 * dm + groups * S5_GROUP
        xn = _resnorm(x, g_pre=ff1_norm_pre[l], want_x=False, n_prompt=n_p)[0]
        w_in_t = jnp.swapaxes(w_in[l], 0, 1)
        d_ff = ff2_w_gate.shape[2]
        later = [(ff2_w_gate[l], 16, d // 16, d_ff), (ff2_w_up[l], 16, d // 16, d_ff),
                 (ff2_w_down[l], 64, d_ff // 64, d), (w_in_t, 32, n_main // 32, d),
                 (w_out[l], 16, d // 16, d), (s5_w_glu[l], 16, s5_w_glu.shape[1] // 16, s5_w_glu.shape[2])]
        y_a, (wg1, wu1, wd1), _ = _ffn(xn, ff1_w_gate[l], ff1_w_up[l], ff1_w_down[l],
                                       n_tiles=1, emit_bf16=True)
        y_b, _, (wg2, wu2, wd2, w_in_b, w_out_b, w_glu_b) = _ffn(xn, wg1, wu1, wd1, side=later, tile0=1,
                                                                 xn_buffers=2)
        x, hmix = _resnorm(x, (y_a, y_b), ff1_norm_post[l], mix_norm_pre[l], alpha=0.5, n_prompt=n_p)

        w_gate_rows = jnp.pad(w_in_t[n_main:], ((0, LANES - 2 * heads), (0, 0)))
        proj, gates = _mm([(hmix, w_in_b, 0)], trans_w=True, narrow_w=w_gate_rows, tm=MM_ROW_TILE)
        gbias = jnp.pad(jnp.concatenate([mlstm_b_i[l], mlstm_b_f[l]]), (0, LANES - 2 * heads)).reshape(1, LANES)

        wq = bf(mlstm_w_q)
        wk = bf(mlstm_w_k)
        convw = mlstm_conv_w[l]
        convb = mlstm_conv_b[l].reshape(1, dm)
        hnorm = mlstm_head_norm[l]
        zc = jnp.zeros((bp, heads, dqk, dv), F32)
        hm_p, c_p, nn_p, m_p = _mlstm(
            proj, gates, 0, bp, lp // t_prompt, t_prompt,
            jnp.zeros((bp, SUBLANES, dm), F32), zc, jnp.zeros((bp, SUBLANES, dqk), F32),
            jnp.zeros((bp, SUBLANES, LANES), F32),
            convw, convb, wq, wk, gbias, hnorm)
        m0_s = jnp.broadcast_to(_pad_rows8(state_mlstm_m[l][:, :, None]), (bs, SUBLANES, LANES))
        hm_s, c_s, nn_s, m_s = _mlstm(
            proj, gates, n_p, bs, 1, ls,
            _pad_rows8(cache_mlstm_conv[l], at_end=True), state_mlstm_C[l],
            _pad_rows8(state_mlstm_n[l]), m0_s, convw, convb, wq, wk, gbias, hnorm)
        hm = jnp.concatenate([hm_p, hm_s], axis=0)
        keep = CONV_W - 1
        conv_p = proj[n_p - keep:n_p, :dm].reshape(bp, keep, dm)
        conv_s = proj[n_p:].reshape(bs, ls, -1)[:, ls - keep:, :dm]

        ys, hp_re, hp_im, hs_re, hs_im = _s5(
            proj, 3 * dm, s5_a_re[l], s5_a_im[l], s5_log_dt[l], s5_b_re[l], s5_b_im[l],
            s5_c_re[l], s5_c_im[l], s5_d[l],
            state_s5_re[l].reshape(bs, groups * st), state_s5_im[l].reshape(bs, groups * st),
            n_prompt_rows=n_p)
        ys = _glu(ys, w_glu_b, s5_b_glu[l])

        assert dm == ys.shape[1]
        mix = _mm([(hm, w_out_b, 0), (ys, w_out_b, 1)], tm=MM_ROW_TILE)
        x, xn = _resnorm(x, mix, mix_norm_post[l], ff2_norm_pre[l], alpha=1.0)
        y = _ffn(xn, wg2, wu2, wd2, xn_buffers=2, out_buffers=2)[0]
        x = _resnorm(x, y, ff2_norm_post[l], alpha=0.5, two_out=last, n_prompt=n_p)
        x = tuple(x) if last else (x[0][:n_p], x[0][n_p:])

        new_p.append((c_p, nn_p[:, :heads], m_p[:, :heads, 0], conv_p,
                      hp_re[:1].reshape(bp, groups, st), hp_im[:1].reshape(bp, groups, st)))
        new_s.append((c_s, nn_s[:, :heads], m_s[:, :heads, 0], conv_s,
                      hs_re.reshape(bs, groups, st), hs_im.reshape(bs, groups, st)))

    outs_p = [jnp.stack([e[i] for e in new_p]) for i in range(6)]
    outs_s = [jnp.stack([e[i] for e in new_s]) for i in range(6)]
    y_prompt = x[0].reshape(bp, lp, d)
    y_sample = x[1].reshape(bs, ls, d)
    return (y_prompt, y_sample, *outs_p, *outs_s)
```
